```python
import math
import jax, jax.numpy as jnp
from jax import lax
import numpy as np

D_MODEL = 1024
BATCH = 8
SEQ = 4096
DEPTH = 1
DEC_BATCH = 8
DEC_SEQ = 8192
PAST_LEN = 128

HEAD_DIM = 64
H_A = 8
H_B = 8
KVH_B = 2
G_B = H_B // KVH_B
W_A = H_A * HEAD_DIM
W_B = H_B * HEAD_DIM
W_KV_B = KVH_B * HEAD_DIM
GRID_W = 64
NA_ROWS = 8
NA_COLS = 16
WINDOW = 128
BLOCK = 128
T5_BUCKETS = 32
T5_MAX_DIST = 128
EPS = 1e-6
SPLITS = (W_A, W_A, W_A, W_A, W_B, W_KV_B, W_KV_B, W_B, D_MODEL, D_MODEL)
D_IN = sum(SPLITS)

kernel_name = "hybrid_natten_window_gqa_gated_encoder"


def rmsnorm(x, g):
    xf = x.astype(jnp.float32)
    y = xf * lax.rsqrt(jnp.mean(xf * xf, axis=-1, keepdims=True) + EPS)
    return (y * g.astype(jnp.float32)).astype(x.dtype)


def t5_bucket(rel):
    half = T5_BUCKETS // 2
    max_exact = half // 2
    n = jnp.abs(rel)
    large = max_exact + (jnp.log(jnp.maximum(n, 1).astype(jnp.float32) / max_exact)
                         / math.log(T5_MAX_DIST / max_exact) * (half - max_exact)).astype(jnp.int32)
    large = jnp.minimum(large, half - 1)
    return jnp.where(rel > 0, half, 0) + jnp.where(n < max_exact, n, large)


def neighbourhood_attention(q, k, v, rpb):
    B, L, H, dh = q.shape
    rows = L // GRID_W
    kr = min(NA_ROWS, rows)
    qg = q.reshape(B, rows, GRID_W, H, dh)
    kg = k.reshape(B, rows, GRID_W, H, dh)
    vg = v.reshape(B, rows, GRID_W, H, dh)
    col = np.arange(GRID_W)
    cs = np.clip(col - NA_COLS // 2, 0, GRID_W - NA_COLS)
    cidx = cs[:, None] + np.arange(NA_COLS)[None, :]
    dc_idx = (cidx - col[:, None]) + (NA_COLS - 1)
    scale = 1.0 / math.sqrt(dh)

    def row_step(args):
        r, q_r = args
        rs = jnp.clip(r - kr // 2, 0, rows - kr)
        k_rows = lax.dynamic_slice_in_dim(kg, rs, kr, axis=1)
        v_rows = lax.dynamic_slice_in_dim(vg, rs, kr, axis=1)
        k_n = k_rows[:, :, cidx]
        v_n = v_rows[:, :, cidx]
        dr_idx = rs + jnp.arange(kr) - r + (NA_ROWS - 1)
        bias = rpb[:, dr_idx[None, :, None], dc_idx[:, None, :]]
        s = jnp.einsum('bqhd,brqchd->bhqrc', q_r, k_n).astype(jnp.float32) * scale
        s = s + bias[None].astype(jnp.float32)
        p = jax.nn.softmax(s.reshape(B, H, GRID_W, kr * NA_COLS), axis=-1)
        p = p.reshape(B, H, GRID_W, kr, NA_COLS).astype(v.dtype)
        return jnp.einsum('bhqrc,brqchd->bqhd', p, v_n)

    q_rows = jnp.moveaxis(qg, 1, 0)
    out = lax.map(row_step, (jnp.arange(rows), q_rows))
    return jnp.moveaxis(out, 0, 1).reshape(B, L, H * dh)


def window_gqa_attention(q, k, v, t5_table, sink):
    B, L, H, dh = q.shape
    nb = L // BLOCK
    pad = ((0, 0), (BLOCK, BLOCK), (0, 0), (0, 0))
    kp = jnp.pad(k, pad).reshape(B, nb + 2, BLOCK, KVH_B, dh)
    vp = jnp.pad(v, pad).reshape(B, nb + 2, BLOCK, KVH_B, dh)
    kb = jnp.concatenate([kp[:, :-2], kp[:, 1:-1], kp[:, 2:]], axis=2)
    vb = jnp.concatenate([vp[:, :-2], vp[:, 1:-1], vp[:, 2:]], axis=2)
    qb = q.reshape(B, nb, BLOCK, KVH_B, G_B, dh)
    scale = 1.0 / math.sqrt(dh)
    s = jnp.einsum('bnqkgd,bnskd->bnkgqs', qb, kb).astype(jnp.float32) * scale
    sidx = jnp.arange(3 * BLOCK)
    rel = sidx[None, :] - BLOCK - jnp.arange(BLOCK)[:, None]
    key_pos = jnp.arange(nb)[:, None] * BLOCK + sidx[None, :] - BLOCK
    mask = (jnp.abs(rel) <= WINDOW)[None] & ((key_pos >= 0) & (key_pos < L))[:, None, :]
    bias = jnp.transpose(t5_table[t5_bucket(rel)], (2, 0, 1)).reshape(KVH_B, G_B, BLOCK, 3 * BLOCK)
    s = s + bias[None, None].astype(jnp.float32)
    s = jnp.where(mask[None, :, None, None], s, jnp.float32(-1e30))
    sink_l = jnp.broadcast_to(sink.astype(jnp.float32).reshape(1, 1, KVH_B, G_B, 1, 1),
                              s.shape[:-1] + (1,))
    p = jax.nn.softmax(jnp.concatenate([s, sink_l], axis=-1), axis=-1)[..., :-1].astype(v.dtype)
    o = jnp.einsum('bnkgqs,bnskd->bnqkgd', p, vb)
    return o.reshape(B, L, H * dh)


def encoder_layer(x, norm_g, w_in, qn_a, kn_a, rpb_a, qn_b, kn_b, sink_b, w_o_a, w_o_b, w_out, t5_table):
    B, L, D = x.shape
    h = rmsnorm(x, norm_g)
    proj = h @ w_in
    qa, ka, va, za, qb, kb, vb, zb, ga, gb = jnp.split(proj, np.cumsum(SPLITS)[:-1], axis=-1)
    qa = rmsnorm(qa.reshape(B, L, H_A, HEAD_DIM), qn_a)
    ka = rmsnorm(ka.reshape(B, L, H_A, HEAD_DIM), kn_a)
    va = va.reshape(B, L, H_A, HEAD_DIM)
    oa = neighbourhood_attention(qa, ka, va, rpb_a) * jax.nn.silu(za)
    oa = oa @ w_o_a
    qb = rmsnorm(qb.reshape(B, L, H_B, HEAD_DIM), qn_b)
    kb = rmsnorm(kb.reshape(B, L, KVH_B, HEAD_DIM), kn_b)
    vb = vb.reshape(B, L, KVH_B, HEAD_DIM)
    ob = window_gqa_attention(qb, kb, vb, t5_table, sink_b) * jax.nn.silu(zb)
    ob = ob @ w_o_b
    merged = jax.nn.sigmoid(ga) * oa + jax.nn.sigmoid(gb) * ob
    return x + merged @ w_out


def setup_inputs(seed: int = 0) -> dict:
    key = jax.random.key(seed)
    ks = jax.random.split(key, 16)
    f32 = jnp.float32
    nrm = lambda k, shape, s: jax.random.normal(k, shape, f32) * s
    return {
        "x_prompt": nrm(ks[0], (BATCH, SEQ, D_MODEL), 1.0),
        "x_sample": nrm(ks[1], (DEC_BATCH, DEC_SEQ, D_MODEL), 1.0),
        "norm_g": 1.0 + nrm(ks[2], (DEPTH, D_MODEL), 0.02),
        "w_in": nrm(ks[3], (DEPTH, D_MODEL, D_IN), D_MODEL ** -0.5),
        "qn_a": 1.0 + nrm(ks[4], (DEPTH, HEAD_DIM), 0.02),
        "kn_a": 1.0 + nrm(ks[5], (DEPTH, HEAD_DIM), 0.02),
        "rpb_a": nrm(ks[6], (DEPTH, H_A, 2 * NA_ROWS - 1, 2 * NA_COLS - 1), 0.1),
        "qn_b": 1.0 + nrm(ks[7], (DEPTH, HEAD_DIM), 0.02),
        "kn_b": 1.0 + nrm(ks[8], (DEPTH, HEAD_DIM), 0.02),
        "sink_b": nrm(ks[9], (DEPTH, H_B), 0.5),
        "w_o_a": nrm(ks[10], (DEPTH, W_A, D_MODEL), W_A ** -0.5),
        "w_o_b": nrm(ks[11], (DEPTH, W_B, D_MODEL), W_B ** -0.5),
        "w_out": nrm(ks[12], (DEPTH, D_MODEL, D_MODEL), D_MODEL ** -0.5),
        "t5_table": nrm(ks[13], (T5_BUCKETS, H_B), 0.1),
    }


def reference(x_prompt, x_sample, norm_g, w_in, qn_a, kn_a, rpb_a, qn_b, kn_b, sink_b,
              w_o_a, w_o_b, w_out, t5_table):
    y_prompt = x_prompt
    y_sample = x_sample
    for l in range(DEPTH):
        params = (norm_g[l], w_in[l], qn_a[l], kn_a[l], rpb_a[l], qn_b[l], kn_b[l], sink_b[l],
                  w_o_a[l], w_o_b[l], w_out[l], t5_table)
        y_prompt = encoder_layer(y_prompt, *params)
        y_sample = encoder_layer(y_sample, *params)
    return (y_prompt, y_sample)
```

```python
import functools
import math

import jax
import jax.numpy as jnp
import numpy as np
from jax import lax
from jax.experimental import pallas as pl
from jax.experimental.pallas import tpu as pltpu

F32 = jnp.float32
BF16 = jnp.bfloat16

D_MODEL = 1024
HEAD_DIM = 64
H_A = 8
H_B = 8
KVH_B = 2
G_B = H_B // KVH_B
W_A = H_A * HEAD_DIM
W_B = H_B * HEAD_DIM
W_KV_B = KVH_B * HEAD_DIM
GRID_W = 64
NA_ROWS = 8
NA_COLS = 16
WINDOW = 128
BLOCK = 128
T5_BUCKETS = 32
T5_MAX_DIST = 128
EPS = 1e-6
NEG = -1e30

LANE_GROUP = 256
HEADS_PER_GROUP = LANE_GROUP // HEAD_DIM
OFF_QA, OFF_KA, OFF_VA, OFF_ZA = 0, W_A, 2 * W_A, 3 * W_A
OFF_QB = 4 * W_A
OFF_KB = OFF_QB + W_B
OFF_VB = OFF_KB + W_KV_B
OFF_ZB = OFF_VB + W_KV_B
OFF_GA = OFF_ZB + W_B
OFF_GB = OFF_GA + D_MODEL
D_IN = OFF_GB + D_MODEL

TM_PROJ = 512
TQ = 256
ROWS_PER_TILE = TQ // GRID_W
KEY_TILES = 3
VMEM_LIMIT = 56 * 1024 * 1024


def _const_spec(shape):
    return pl.BlockSpec(shape, lambda *_: (0,) * len(shape))


def _proj_kernel(x_ref, g_ref, w_ref, bd_ref, rep_ref, gains_ref,
                 qa_ref, ka_ref, va_ref, za_ref, qb_ref, kb_ref, vb_ref, zb_ref, ga_ref, gb_ref):
    x = x_ref[...]
    ms = jnp.mean(x * x, axis=-1, keepdims=True)
    h = (x * lax.rsqrt(ms + EPS) * g_ref[...]).astype(BF16)

    def proj(lo, width):
        return jnp.dot(h, w_ref[:, lo:lo + width], preferred_element_type=F32)

    def head_norm(p, gain):
        width = p.shape[-1]
        sq = (p * p).astype(BF16)
        parts = []
        for lo in range(0, width, LANE_GROUP):
            w = min(LANE_GROUP, width - lo)
            parts.append(jnp.dot(sq[:, lo:lo + w], bd_ref[:w, :w], preferred_element_type=F32))
        ssq = parts[0] if len(parts) == 1 else jnp.concatenate(parts, axis=-1)
        return p * lax.rsqrt(ssq * (1.0 / HEAD_DIM) + EPS) * gain

    def silu(z):
        return z * jax.nn.sigmoid(z)

    qa_ref[...] = head_norm(proj(OFF_QA, W_A), gains_ref[0:1, :]).astype(BF16)
    ka_ref[...] = head_norm(proj(OFF_KA, W_A), gains_ref[1:2, :]).astype(BF16)
    va_ref[...] = proj(OFF_VA, W_A).astype(BF16)
    za_ref[...] = silu(proj(OFF_ZA, W_A)).astype(BF16)
    qb_ref[...] = head_norm(proj(OFF_QB, W_B), gains_ref[2:3, :]).astype(BF16)
    kb = head_norm(proj(OFF_KB, W_KV_B), gains_ref[3:4, :W_KV_B]).astype(BF16)
    kb_ref[...] = jnp.dot(kb, rep_ref[...], preferred_element_type=F32).astype(BF16)
    vb = proj(OFF_VB, W_KV_B).astype(BF16)
    vb_ref[...] = jnp.dot(vb, rep_ref[...], preferred_element_type=F32).astype(BF16)
    zb_ref[...] = silu(proj(OFF_ZB, W_B)).astype(BF16)
    for c in range(0, D_MODEL, W_A):
        ga_ref[:, c:c + W_A] = jax.nn.sigmoid(proj(OFF_GA + c, W_A)).astype(BF16)
        gb_ref[:, c:c + W_A] = jax.nn.sigmoid(proj(OFF_GB + c, W_A)).astype(BF16)


def _project(x2d, g_row, w_bf, bd, rep, gains):
    n_tok = x2d.shape[0]
    tm = TM_PROJ
    assert n_tok % tm == 0
    tok = lambda width: pl.BlockSpec((tm, width), lambda i: (i, 0))
    widths = (W_A, W_A, W_A, W_A, W_B, W_B, W_B, W_B, D_MODEL, D_MODEL)
    return pl.pallas_call(
        _proj_kernel,
        grid=(n_tok // tm,),
        in_specs=[tok(D_MODEL), _const_spec(g_row.shape), _const_spec(w_bf.shape),
                  _const_spec(bd.shape), _const_spec(rep.shape), _const_spec(gains.shape)],
        out_specs=[tok(w) for w in widths],
        out_shape=[jax.ShapeDtypeStruct((n_tok, w), BF16) for w in widths],
        compiler_params=pltpu.CompilerParams(dimension_semantics=("arbitrary",),
                                             vmem_limit_bytes=VMEM_LIMIT),
        name="proj",
    )(x2d, g_row, w_bf, bd, rep, gains)


def _halo_specs(nt, width):
    prev = pl.BlockSpec((TQ, width), lambda b, t: (b * nt + jnp.maximum(t - 1, 0), 0))
    cur = pl.BlockSpec((TQ, width), lambda b, t: (b * nt + t, 0))
    nxt = pl.BlockSpec((TQ, width), lambda b, t: (b * nt + jnp.minimum(t + 1, nt - 1), 0))
    return [prev, cur, nxt]


def _tile_case(nt):
    t = pl.program_id(1)
    return jnp.where(t == 0, 0, jnp.where(t == nt - 1, 2, 1))


def _head_masks():
    lane_head = lax.broadcasted_iota(jnp.int32, (1, LANE_GROUP), 1) // HEAD_DIM
    return [(lane_head == j).astype(F32) for j in range(HEADS_PER_GROUP)]


def _na_kernel(q_ref, kp_ref, kc_ref, kn_ref, vp_ref, vc_ref, vn_ref, z_ref, bias_ref, o_ref, *, nt):
    case = _tile_case(nt)
    masks = _head_masks()
    for g in range(W_A // LANE_GROUP):
        cols = slice(g * LANE_GROUP, (g + 1) * LANE_GROUP)
        q4 = q_ref[:, cols].astype(F32)
        k4 = jnp.concatenate([kp_ref[:, cols], kc_ref[:, cols], kn_ref[:, cols]], axis=0)
        v4 = jnp.concatenate([vp_ref[:, cols], vc_ref[:, cols], vn_ref[:, cols]], axis=0)
        o4 = jnp.zeros((TQ, LANE_GROUP), F32)
        for j in range(HEADS_PER_GROUP):
            qm = (q4 * masks[j]).astype(BF16)
            s = lax.dot_general(qm, k4, (((1,), (1,)), ((), ())), preferred_element_type=F32)
            s = s + bias_ref[case, g * HEADS_PER_GROUP + j]
            m = jnp.max(s, axis=-1, keepdims=True)
            p = jnp.exp(s - m)
            l = jnp.sum(p, axis=-1, keepdims=True)
            pv = jnp.dot(p.astype(BF16), v4, preferred_element_type=F32)
            o4 = o4 + pv * ((1.0 / l) * masks[j])
        o_ref[:, cols] = (o4 * z_ref[:, cols].astype(F32)).astype(BF16)


def _na_attention(q, k, v, z, bias, batch, seq):
    nt = seq // TQ
    assert seq % TQ == 0 and nt >= 3
    halo = _halo_specs(nt, W_A)
    cur = halo[1]
    return pl.pallas_call(
        functools.partial(_na_kernel, nt=nt),
        grid=(batch, nt),
        in_specs=[cur] + halo + halo + [cur, _const_spec(bias.shape)],
        out_specs=cur,
        out_shape=jax.ShapeDtypeStruct((batch * seq, W_A), BF16),
        compiler_params=pltpu.CompilerParams(dimension_semantics=("arbitrary", "arbitrary"),
                                             vmem_limit_bytes=VMEM_LIMIT),
        name="na_attn",
    )(q, k, k, k, v, v, v, z, bias)


def _na_bias(rpb, rows):
    kr = min(NA_ROWS, rows)
    assert kr == NA_ROWS and rows >= 3 * ROWS_PER_TILE
    n_key_rows = KEY_TILES * ROWS_PER_TILE
    i = np.arange(ROWS_PER_TILE)[:, None, None, None]
    qc = np.arange(GRID_W)[None, :, None, None]
    j = np.arange(n_key_rows)[None, None, :, None]
    kc = np.arange(GRID_W)[None, None, None, :]
    cs = np.clip(qc - NA_COLS // 2, 0, GRID_W - NA_COLS)
    col_ok = (kc >= cs) & (kc < cs + NA_COLS)
    dr = j - ROWS_PER_TILE - i
    dc = kc - qc
    shape = (ROWS_PER_TILE, GRID_W, n_key_rows, GRID_W)
    dr_idx = np.broadcast_to(np.clip(dr + NA_ROWS - 1, 0, 2 * NA_ROWS - 2), shape).reshape(TQ, KEY_TILES * TQ)
    dc_idx = np.broadcast_to(np.clip(dc + NA_COLS - 1, 0, 2 * NA_COLS - 2), shape).reshape(TQ, KEY_TILES * TQ)
    vals = rpb[:, dr_idx, dc_idx].astype(F32)
    row_ok = [
        (j >= ROWS_PER_TILE) & (j < ROWS_PER_TILE + kr),
        (j - i >= 0) & (j - i < kr),
        (j >= 0) & (j < kr),
    ]
    out = []
    for ok in row_ok:
        valid = np.broadcast_to(ok & col_ok, shape).reshape(TQ, KEY_TILES * TQ)
        out.append(jnp.where(valid[None], vals, NEG))
    return jnp.stack(out)


def _win_kernel(q_ref, kp_ref, kc_ref, kn_ref, vp_ref, vc_ref, vn_ref, z_ref, bias_ref, sink_ref, o_ref, *, nt):
    t = pl.program_id(1)
    masks = _head_masks()
    n_sub = TQ // BLOCK
    for g in range(KVH_B):
        cols = slice(g * LANE_GROUP, (g + 1) * LANE_GROUP)
        kwin = jnp.concatenate([kp_ref[:, cols], kc_ref[:, cols], kn_ref[:, cols]], axis=0)
        vwin = jnp.concatenate([vp_ref[:, cols], vc_ref[:, cols], vn_ref[:, cols]], axis=0)
        sink = sink_ref[g]
        for sub in range(n_sub):
            rows = slice(sub * BLOCK, (sub + 1) * BLOCK)
            key_lo = TQ + (sub - 1) * BLOCK
            k3 = kwin[key_lo:key_lo + 3 * BLOCK]
            v3 = vwin[key_lo:key_lo + 3 * BLOCK]
            if sub == 0:
                case = jnp.where(t == 0, 0, 1)
            elif sub == n_sub - 1:
                case = jnp.where(t == nt - 1, 2, 1)
            else:
                case = 1
            q4 = q_ref[rows, cols].astype(F32)
            qcat = jnp.concatenate([(q4 * masks[j]).astype(BF16) for j in range(G_B)], axis=0)
            s = lax.dot_general(qcat, k3, (((1,), (1,)), ((), ())), preferred_element_type=F32)
            s = s + bias_ref[case, g]
            m = jnp.maximum(jnp.max(s, axis=-1, keepdims=True), sink)
            p = jnp.exp(s - m)
            l = jnp.sum(p, axis=-1, keepdims=True) + jnp.exp(sink - m)
            pv = jnp.dot(p.astype(BF16), v3, preferred_element_type=F32) * (1.0 / l)
            o4 = pv[0:BLOCK] * masks[0]
            for j in range(1, G_B):
                o4 = o4 + pv[j * BLOCK:(j + 1) * BLOCK] * masks[j]
            o_ref[rows, cols] = (o4 * z_ref[rows, cols].astype(F32)).astype(BF16)


def _win_attention(q, k, v, z, bias, sink, batch, seq):
    nt = seq // TQ
    assert seq % TQ == 0 and nt >= 2
    halo = _halo_specs(nt, W_B)
    cur = halo[1]
    return pl.pallas_call(
        functools.partial(_win_kernel, nt=nt),
        grid=(batch, nt),
        in_specs=[cur] + halo + halo + [cur, _const_spec(bias.shape), _const_spec(sink.shape)],
        out_specs=cur,
        out_shape=jax.ShapeDtypeStruct((batch * seq, W_B), BF16),
        compiler_params=pltpu.CompilerParams(dimension_semantics=("arbitrary", "arbitrary"),
                                             vmem_limit_bytes=VMEM_LIMIT),
        name="win_attn",
    )(q, k, k, k, v, v, v, z, bias, sink)


def _t5_bucket(rel):
    half = T5_BUCKETS // 2
    max_exact = half // 2
    n = jnp.abs(rel)
    large = max_exact + (jnp.log(jnp.maximum(n, 1).astype(jnp.float32) / max_exact)
                         / math.log(T5_MAX_DIST / max_exact) * (half - max_exact)).astype(jnp.int32)
    large = jnp.minimum(large, half - 1)
    return jnp.where(rel > 0, half, 0) + jnp.where(n < max_exact, n, large)


def _win_bias(t5_table):
    sidx = jnp.arange(3 * BLOCK)
    rel = sidx[None, :] - BLOCK - jnp.arange(BLOCK)[:, None]
    bias = jnp.transpose(t5_table[_t5_bucket(rel)], (2, 0, 1)).astype(F32)
    band = (jnp.abs(rel) <= WINDOW)[None]
    blk = sidx // BLOCK
    cases = [band & (blk != 0)[None, None], band, band & (blk != 2)[None, None]]
    out = [jnp.where(c, bias, NEG).reshape(KVH_B, G_B * BLOCK, 3 * BLOCK) for c in cases]
    return jnp.stack(out)


def _out_kernel(x_ref, oa_ref, ob_ref, ga_ref, gb_ref, woa_ref, wob_ref, wout_ref, y_ref):
    a = jnp.dot(oa_ref[...], woa_ref[...], preferred_element_type=F32)
    b = jnp.dot(ob_ref[...], wob_ref[...], preferred_element_type=F32)
    merged = ga_ref[...].astype(F32) * a + gb_ref[...].astype(F32) * b
    y_ref[...] = x_ref[...] + jnp.dot(merged.astype(BF16), wout_ref[...], preferred_element_type=F32)


def _out_project(x2d, oa, ob, ga, gb, woa, wob, wout):
    n_tok = x2d.shape[0]
    tm = TM_PROJ
    tok = lambda width: pl.BlockSpec((tm, width), lambda i: (i, 0))
    return pl.pallas_call(
        _out_kernel,
        grid=(n_tok // tm,),
        in_specs=[tok(D_MODEL), tok(W_A), tok(W_B), tok(D_MODEL), tok(D_MODEL),
                  _const_spec(woa.shape), _const_spec(wob.shape), _const_spec(wout.shape)],
        out_specs=tok(D_MODEL),
        out_shape=jax.ShapeDtypeStruct((n_tok, D_MODEL), F32),
        compiler_params=pltpu.CompilerParams(dimension_semantics=("arbitrary",),
                                             vmem_limit_bytes=VMEM_LIMIT),
        name="out_proj",
    )(x2d, oa, ob, ga, gb, woa, wob, wout)


def _layer_constants(qn_a, kn_a, qn_b, kn_b, sink_b):
    scale = 1.0 / math.sqrt(HEAD_DIM)
    tile = lambda gvec, s: jnp.tile(gvec.astype(F32) * s, W_A // HEAD_DIM)
    gains = jnp.stack([tile(qn_a, scale), tile(kn_a, 1.0), tile(qn_b, scale), tile(kn_b, 1.0)])
    lane = np.arange(LANE_GROUP)
    bd = jnp.asarray(lane[:, None] // HEAD_DIM == lane[None, :] // HEAD_DIM, BF16)
    src = np.arange(W_KV_B)[:, None]
    dst = np.arange(W_B)[None, :]
    rep = jnp.asarray(src == HEAD_DIM * (dst // LANE_GROUP) + dst % HEAD_DIM, BF16)
    sink = jnp.repeat(sink_b.astype(F32), BLOCK).reshape(KVH_B, G_B * BLOCK, 1)
    return gains, bd, rep, sink


def _encoder_layer(x, norm_g, w_in, qn_a, kn_a, rpb_a, qn_b, kn_b, sink_b, w_o_a, w_o_b, w_out, t5_table):
    batch, seq, d = x.shape
    assert d == D_MODEL and seq % GRID_W == 0
    x2d = x.reshape(batch * seq, d)
    gains, bd, rep, sink = _layer_constants(qn_a, kn_a, qn_b, kn_b, sink_b)
    g_row = norm_g.astype(F32).reshape(1, d)
    qa, ka, va, za, qb, kb, vb, zb, ga, gb = _project(x2d, g_row, w_in.astype(BF16), bd, rep, gains)
    oa = _na_attention(qa, ka, va, za, _na_bias(rpb_a, seq // GRID_W), batch, seq)
    ob = _win_attention(qb, kb, vb, zb, _win_bias(t5_table), sink, batch, seq)
    y = _out_project(x2d, oa, ob, ga, gb, w_o_a.astype(BF16), w_o_b.astype(BF16), w_out.astype(BF16))
    return y.reshape(batch, seq, d)


def kernel(x_prompt, x_sample, norm_g, w_in, qn_a, kn_a, rpb_a, qn_b, kn_b, sink_b, w_o_a, w_o_b, w_out, t5_table):
    y_prompt, y_sample = x_prompt, x_sample
    for l in range(norm_g.shape[0]):
        params = (norm_g[l], w_in[l], qn_a[l], kn_a[l], rpb_a[l], qn_b[l], kn_b[l], sink_b[l],
                  w_o_a[l], w_o_b[l], w_out[l], t5_table)
        y_prompt = _encoder_layer(y_prompt, *params)
        y_sample = _encoder_layer(y_sample, *params)
    return (y_prompt, y_sample)
```

```python
import functools
import math

import jax
import jax.numpy as jnp
import numpy as np
from jax import lax
from jax.experimental import pallas as pl
from jax.experimental.pallas import tpu as pltpu

F32 = jnp.float32
BF16 = jnp.bfloat16

D_MODEL = 1024
HEAD_DIM = 64
H_A = 8
H_B = 8
KVH_B = 2
G_B = H_B // KVH_B
W_A = H_A * HEAD_DIM
W_B = H_B * HEAD_DIM
W_KV_B = KVH_B * HEAD_DIM
GRID_W = 64
NA_ROWS = 8
NA_COLS = 16
WINDOW = 128
BLOCK = 128
T5_BUCKETS = 32
T5_MAX_DIST = 128
EPS = 1e-6
NEG = -1e30

LANE_GROUP = 256
HEADS_PER_GROUP = LANE_GROUP // HEAD_DIM
OFF_QA, OFF_KA, OFF_VA, OFF_ZA = 0, W_A, 2 * W_A, 3 * W_A
OFF_QB = 4 * W_A
OFF_KB = OFF_QB + W_B
OFF_VB = OFF_KB + W_KV_B
OFF_ZB = OFF_VB + W_KV_B
OFF_GA = OFF_ZB + W_B
OFF_GB = OFF_GA + D_MODEL
D_IN = OFF_GB + D_MODEL

TM_PROJ = 512
TQ = 256
ROWS_PER_TILE = TQ // GRID_W
KEY_TILES = 3
WIN_STACK = 2
VMEM_LIMIT = 56 * 1024 * 1024


def _const_spec(shape):
    return pl.BlockSpec(shape, lambda *_: (0,) * len(shape))


def _proj_kernel(x_ref, g_ref, w_ref, bd_ref, rep_ref, gains_ref,
                 qa_ref, ka_ref, va_ref, za_ref, qb_ref, kb_ref, vb_ref, zb_ref, ga_ref, gb_ref):
    x = x_ref[...]
    ms = jnp.mean(x * x, axis=-1, keepdims=True)
    h = (x * lax.rsqrt(ms + EPS) * g_ref[...]).astype(BF16)

    def proj(lo, width):
        return jnp.dot(h, w_ref[:, lo:lo + width], preferred_element_type=F32)

    def head_norm(p, gain):
        width = p.shape[-1]
        sq = (p * p).astype(BF16)
        parts = []
        for lo in range(0, width, LANE_GROUP):
            w = min(LANE_GROUP, width - lo)
            parts.append(jnp.dot(sq[:, lo:lo + w], bd_ref[:w, :w], preferred_element_type=F32))
        ssq = parts[0] if len(parts) == 1 else jnp.concatenate(parts, axis=-1)
        return p * lax.rsqrt(ssq * (1.0 / HEAD_DIM) + EPS) * gain

    def silu(z):
        return z * jax.nn.sigmoid(z)

    qa_ref[...] = head_norm(proj(OFF_QA, W_A), gains_ref[0:1, :]).astype(BF16)
    ka_ref[...] = head_norm(proj(OFF_KA, W_A), gains_ref[1:2, :]).astype(BF16)
    va_ref[...] = proj(OFF_VA, W_A).astype(BF16)
    za_ref[...] = silu(proj(OFF_ZA, W_A)).astype(BF16)
    qb_ref[...] = head_norm(proj(OFF_QB, W_B), gains_ref[2:3, :]).astype(BF16)
    kb = head_norm(proj(OFF_KB, W_KV_B), gains_ref[3:4, :W_KV_B]).astype(BF16)
    kb_ref[...] = jnp.dot(kb, rep_ref[...], preferred_element_type=F32).astype(BF16)
    vb = proj(OFF_VB, W_KV_B).astype(BF16)
    vb_ref[...] = jnp.dot(vb, rep_ref[...], preferred_element_type=F32).astype(BF16)
    zb_ref[...] = silu(proj(OFF_ZB, W_B)).astype(BF16)
    for c in range(0, D_MODEL, W_A):
        ga_ref[:, c:c + W_A] = jax.nn.sigmoid(proj(OFF_GA + c, W_A)).astype(BF16)
        gb_ref[:, c:c + W_A] = jax.nn.sigmoid(proj(OFF_GB + c, W_A)).astype(BF16)


def _project(x2d, g_row, w_bf, bd, rep, gains):
    n_tok = x2d.shape[0]
    tm = TM_PROJ
    assert n_tok % tm == 0
    tok = lambda width: pl.BlockSpec((tm, width), lambda i: (i, 0))
    widths = (W_A, W_A, W_A, W_A, W_B, W_B, W_B, W_B, D_MODEL, D_MODEL)
    return pl.pallas_call(
        _proj_kernel,
        grid=(n_tok // tm,),
        in_specs=[tok(D_MODEL), _const_spec(g_row.shape), _const_spec(w_bf.shape),
                  _const_spec(bd.shape), _const_spec(rep.shape), _const_spec(gains.shape)],
        out_specs=[tok(w) for w in widths],
        out_shape=[jax.ShapeDtypeStruct((n_tok, w), BF16) for w in widths],
        compiler_params=pltpu.CompilerParams(dimension_semantics=("arbitrary",),
                                             vmem_limit_bytes=VMEM_LIMIT),
        name="proj",
    )(x2d, g_row, w_bf, bd, rep, gains)


def _halo_specs(nt, width):
    prev = pl.BlockSpec((TQ, width), lambda b, t: (b * nt + jnp.maximum(t - 1, 0), 0))
    cur = pl.BlockSpec((TQ, width), lambda b, t: (b * nt + t, 0))
    nxt = pl.BlockSpec((TQ, width), lambda b, t: (b * nt + jnp.minimum(t + 1, nt - 1), 0))
    return [prev, cur, nxt]


def _tile_case(nt):
    t = pl.program_id(1)
    return jnp.where(t == 0, 0, jnp.where(t == nt - 1, 2, 1))


def _head_masks():
    lane_head = lax.broadcasted_iota(jnp.int32, (1, LANE_GROUP), 1) // HEAD_DIM
    return [(lane_head == j).astype(F32) for j in range(HEADS_PER_GROUP)]


def _na_kernel(q_ref, kp_ref, kc_ref, kn_ref, vp_ref, vc_ref, vn_ref, z_ref, bias_ref, o_ref, *, nt):
    case = _tile_case(nt)
    masks = _head_masks()
    for g in range(W_A // LANE_GROUP):
        cols = slice(g * LANE_GROUP, (g + 1) * LANE_GROUP)
        q4 = q_ref[:, cols].astype(F32)
        k4 = jnp.concatenate([kp_ref[:, cols], kc_ref[:, cols], kn_ref[:, cols]], axis=0)
        v4 = jnp.concatenate([vp_ref[:, cols], vc_ref[:, cols], vn_ref[:, cols]], axis=0)
        o4 = jnp.zeros((TQ, LANE_GROUP), F32)
        for j in range(HEADS_PER_GROUP):
            qm = (q4 * masks[j]).astype(BF16)
            s = lax.dot_general(qm, k4, (((1,), (1,)), ((), ())), preferred_element_type=F32)
            s = s + bias_ref[case, g * HEADS_PER_GROUP + j]
            m = jnp.max(s, axis=-1, keepdims=True)
            p = jnp.exp(s - m)
            l = jnp.sum(p, axis=-1, keepdims=True)
            pv = jnp.dot(p.astype(BF16), v4, preferred_element_type=F32)
            o4 = o4 + pv * ((1.0 / l) * masks[j])
        o_ref[:, cols] = (o4 * z_ref[:, cols].astype(F32)).astype(BF16)


def _na_attention(q, k, v, z, bias, batch, seq):
    nt = seq // TQ
    assert seq % TQ == 0 and nt >= 3
    halo = _halo_specs(nt, W_A)
    cur = halo[1]
    return pl.pallas_call(
        functools.partial(_na_kernel, nt=nt),
        grid=(batch, nt),
        in_specs=[cur] + halo + halo + [cur, _const_spec(bias.shape)],
        out_specs=cur,
        out_shape=jax.ShapeDtypeStruct((batch * seq, W_A), BF16),
        compiler_params=pltpu.CompilerParams(dimension_semantics=("arbitrary", "arbitrary"),
                                             vmem_limit_bytes=VMEM_LIMIT),
        name="na_attn",
    )(q, k, k, k, v, v, v, z, bias)


def _toeplitz(v, m, n):
    w = v.shape[-1]
    assert n <= w - 1
    flat = jnp.tile(v, m)[..., :m * (w - 1)]
    return flat.reshape(v.shape[:-1] + (m, w - 1))[..., :n]


def _na_bias(rpb, rows):
    kr = min(NA_ROWS, rows)
    assert kr == NA_ROWS and rows >= 3 * ROWS_PER_TILE
    n_key_rows = KEY_TILES * ROWS_PER_TILE
    rpb = rpb.astype(F32)
    ring_pad = jnp.zeros(rpb.shape[:-1] + (2 * GRID_W - (2 * NA_COLS - 1),), F32)
    ring = jnp.concatenate([rpb[..., NA_COLS - 1:], ring_pad, rpb[..., :NA_COLS - 1]], axis=-1)
    col = _toeplitz(ring, GRID_W, GRID_W)
    first = NA_ROWS - 1 - ROWS_PER_TILE
    per_row = [col[:, first - i:first - i + n_key_rows] for i in range(ROWS_PER_TILE)]
    vals = jnp.transpose(jnp.stack(per_row, axis=1), (0, 1, 3, 2, 4)).reshape(H_A, TQ, KEY_TILES * TQ)
    i = np.arange(ROWS_PER_TILE)[:, None, None, None]
    qc = np.arange(GRID_W)[None, :, None, None]
    j = np.arange(n_key_rows)[None, None, :, None]
    kc = np.arange(GRID_W)[None, None, None, :]
    cs = np.clip(qc - NA_COLS // 2, 0, GRID_W - NA_COLS)
    col_ok = (kc >= cs) & (kc < cs + NA_COLS)
    shape = (ROWS_PER_TILE, GRID_W, n_key_rows, GRID_W)
    row_ok = [
        (j >= ROWS_PER_TILE) & (j < ROWS_PER_TILE + kr),
        (j - i >= 0) & (j - i < kr),
        (j >= 0) & (j < kr),
    ]
    out = []
    for ok in row_ok:
        valid = np.broadcast_to(ok & col_ok, shape).reshape(TQ, KEY_TILES * TQ)
        out.append(jnp.where(valid[None], vals, NEG))
    return jnp.stack(out)


def _win_kernel(q_ref, kp_ref, kc_ref, kn_ref, vp_ref, vc_ref, vn_ref, z_ref, bias_ref, sink_ref, o_ref, *, nt):
    t = pl.program_id(1)
    masks = _head_masks()
    n_sub = TQ // BLOCK
    for g in range(KVH_B):
        cols = slice(g * LANE_GROUP, (g + 1) * LANE_GROUP)
        kwin = jnp.concatenate([kp_ref[:, cols], kc_ref[:, cols], kn_ref[:, cols]], axis=0)
        vwin = jnp.concatenate([vp_ref[:, cols], vc_ref[:, cols], vn_ref[:, cols]], axis=0)
        for sub in range(n_sub):
            rows = slice(sub * BLOCK, (sub + 1) * BLOCK)
            key_lo = TQ + (sub - 1) * BLOCK
            k3 = kwin[key_lo:key_lo + 3 * BLOCK]
            v3 = vwin[key_lo:key_lo + 3 * BLOCK]
            if sub == 0:
                case = jnp.where(t == 0, 0, 1)
            elif sub == n_sub - 1:
                case = jnp.where(t == nt - 1, 2, 1)
            else:
                case = 1
            q4 = q_ref[rows, cols].astype(F32)
            o4 = jnp.zeros((BLOCK, LANE_GROUP), F32)
            for j0 in range(0, G_B, WIN_STACK):
                heads = range(j0, j0 + WIN_STACK)
                srows = slice(j0 * BLOCK, (j0 + WIN_STACK) * BLOCK)
                sink = sink_ref[g, srows]
                qcat = jnp.concatenate([(q4 * masks[j]).astype(BF16) for j in heads], axis=0)
                s = lax.dot_general(qcat, k3, (((1,), (1,)), ((), ())), preferred_element_type=F32)
                s = s + bias_ref[case, g, srows]
                m = jnp.maximum(jnp.max(s, axis=-1, keepdims=True), sink)
                p = jnp.exp(s - m)
                l = jnp.sum(p, axis=-1, keepdims=True) + jnp.exp(sink - m)
                pv = jnp.dot(p.astype(BF16), v3, preferred_element_type=F32) * (1.0 / l)
                for i, j in enumerate(heads):
                    o4 = o4 + pv[i * BLOCK:(i + 1) * BLOCK] * masks[j]
            o_ref[rows, cols] = (o4 * z_ref[rows, cols].astype(F32)).astype(BF16)


def _win_attention(q, k, v, z, bias, sink, batch, seq):
    nt = seq // TQ
    assert seq % TQ == 0 and nt >= 2
    halo = _halo_specs(nt, W_B)
    cur = halo[1]
    return pl.pallas_call(
        functools.partial(_win_kernel, nt=nt),
        grid=(batch, nt),
        in_specs=[cur] + halo + halo + [cur, _const_spec(bias.shape), _const_spec(sink.shape)],
        out_specs=cur,
        out_shape=jax.ShapeDtypeStruct((batch * seq, W_B), BF16),
        compiler_params=pltpu.CompilerParams(dimension_semantics=("arbitrary", "arbitrary"),
                                             vmem_limit_bytes=VMEM_LIMIT),
        name="win_attn",
    )(q, k, k, k, v, v, v, z, bias, sink)


def _t5_bucket(rel):
    half = T5_BUCKETS // 2
    max_exact = half // 2
    n = jnp.abs(rel)
    large = max_exact + (jnp.log(jnp.maximum(n, 1).astype(jnp.float32) / max_exact)
                         / math.log(T5_MAX_DIST / max_exact) * (half - max_exact)).astype(jnp.int32)
    large = jnp.minimum(large, half - 1)
    return jnp.where(rel > 0, half, 0) + jnp.where(n < max_exact, n, large)


def _win_bias(t5_table):
    ring = 4 * BLOCK
    u = jnp.arange(ring)
    rel = jnp.where(u < ring // 2, u, u - ring)
    onehot = (_t5_bucket(rel)[:, None] == jnp.arange(T5_BUCKETS)[None, :]).astype(F32)
    by_rel = jnp.dot(onehot, t5_table.astype(F32), precision=lax.Precision.HIGHEST)
    by_rel = jnp.where((jnp.abs(rel) <= WINDOW)[:, None], by_rel, NEG).T
    bias = _toeplitz(jnp.roll(by_rel, BLOCK, axis=-1), BLOCK, 3 * BLOCK)
    blk = np.arange(3 * BLOCK) // BLOCK
    cases = [jnp.where((blk != 0)[None, None], bias, NEG), bias, jnp.where((blk != 2)[None, None], bias, NEG)]
    return jnp.stack([c.reshape(KVH_B, G_B * BLOCK, 3 * BLOCK) for c in cases])


def _out_kernel(x_ref, oa_ref, ob_ref, ga_ref, gb_ref, woa_ref, wob_ref, wout_ref, y_ref):
    a = jnp.dot(oa_ref[...], woa_ref[...], preferred_element_type=F32)
    b = jnp.dot(ob_ref[...], wob_ref[...], preferred_element_type=F32)
    merged = ga_ref[...].astype(F32) * a + gb_ref[...].astype(F32) * b
    y_ref[...] = x_ref[...] + jnp.dot(merged.astype(BF16), wout_ref[...], preferred_element_type=F32)


def _out_project(x2d, oa, ob, ga, gb, woa, wob, wout):
    n_tok = x2d.shape[0]
    tm = TM_PROJ
    tok = lambda width: pl.BlockSpec((tm, width), lambda i: (i, 0))
    return pl.pallas_call(
        _out_kernel,
        grid=(n_tok // tm,),
        in_specs=[tok(D_MODEL), tok(W_A), tok(W_B), tok(D_MODEL), tok(D_MODEL),
                  _const_spec(woa.shape), _const_spec(wob.shape), _const_spec(wout.shape)],
        out_specs=tok(D_MODEL),
        out_shape=jax.ShapeDtypeStruct((n_tok, D_MODEL), F32),
        compiler_params=pltpu.CompilerParams(dimension_semantics=("arbitrary",),
                                             vmem_limit_bytes=VMEM_LIMIT),
        name="out_proj",
    )(x2d, oa, ob, ga, gb, woa, wob, wout)


def _layer_constants(qn_a, kn_a, qn_b, kn_b, sink_b):
    scale = 1.0 / math.sqrt(HEAD_DIM)
    tile = lambda gvec, s: jnp.tile(gvec.astype(F32) * s, W_A // HEAD_DIM)
    gains = jnp.stack([tile(qn_a, scale), tile(kn_a, 1.0), tile(qn_b, scale), tile(kn_b, 1.0)])
    lane = np.arange(LANE_GROUP)
    bd = jnp.asarray(lane[:, None] // HEAD_DIM == lane[None, :] // HEAD_DIM, BF16)
    src = np.arange(W_KV_B)[:, None]
    dst = np.arange(W_B)[None, :]
    rep = jnp.asarray(src == HEAD_DIM * (dst // LANE_GROUP) + dst % HEAD_DIM, BF16)
    sink = jnp.repeat(sink_b.astype(F32), BLOCK).reshape(KVH_B, G_B * BLOCK, 1)
    return gains, bd, rep, sink


def _encoder_layer(x, norm_g, w_in, qn_a, kn_a, rpb_a, qn_b, kn_b, sink_b, w_o_a, w_o_b, w_out, t5_table):
    batch, seq, d = x.shape
    assert d == D_MODEL and seq % GRID_W == 0
    x2d = x.reshape(batch * seq, d)
    gains, bd, rep, sink = _layer_constants(qn_a, kn_a, qn_b, kn_b, sink_b)
    g_row = norm_g.astype(F32).reshape(1, d)
    qa, ka, va, za, qb, kb, vb, zb, ga, gb = _project(x2d, g_row, w_in.astype(BF16), bd, rep, gains)
    oa = _na_attention(qa, ka, va, za, _na_bias(rpb_a, seq // GRID_W), batch, seq)
    ob = _win_attention(qb, kb, vb, zb, _win_bias(t5_table), sink, batch, seq)
    y = _out_project(x2d, oa, ob, ga, gb, w_o_a.astype(BF16), w_o_b.astype(BF16), w_out.astype(BF16))
    return y.reshape(batch, seq, d)


def kernel(x_prompt, x_sample, norm_g, w_in, qn_a, kn_a, rpb_a, qn_b, kn_b, sink_b, w_o_a, w_o_b, w_out, t5_table):
    y_prompt, y_sample = x_prompt, x_sample
    for l in range(norm_g.shape[0]):
        params = (norm_g[l], w_in[l], qn_a[l], kn_a[l], rpb_a[l], qn_b[l], kn_b[l], sink_b[l],
                  w_o_a[l], w_o_b[l], w_out[l], t5_table)
        y_prompt = _encoder_layer(y_prompt, *params)
        y_sample = _encoder_layer(y_sample, *params)
    return (y_prompt, y_sample)
```

```python
import functools
import math

import jax
import jax.numpy as jnp
import numpy as np
from jax import lax
from jax.experimental import pallas as pl
from jax.experimental.pallas import tpu as pltpu

F32 = jnp.float32
BF16 = jnp.bfloat16

D_MODEL = 1024
HEAD_DIM = 64
H_A = 8
H_B = 8
KVH_B = 2
G_B = H_B // KVH_B
W_A = H_A * HEAD_DIM
W_B = H_B * HEAD_DIM
W_KV_B = KVH_B * HEAD_DIM
GRID_W = 64
NA_ROWS = 8
NA_COLS = 16
WINDOW = 128
BLOCK = 128
T5_BUCKETS = 32
T5_MAX_DIST = 128
EPS = 1e-6
NEG = -1e30
LOG2E = math.log2(math.e)

LANE_GROUP = 256
HEADS_PER_GROUP = LANE_GROUP // HEAD_DIM
OFF_QA, OFF_KA, OFF_VA, OFF_ZA = 0, W_A, 2 * W_A, 3 * W_A
OFF_QB = 4 * W_A
OFF_KB = OFF_QB + W_B
OFF_VB = OFF_KB + W_KV_B
OFF_ZB = OFF_VB + W_KV_B
OFF_GA = OFF_ZB + W_B
OFF_GB = OFF_GA + D_MODEL
D_IN = OFF_GB + D_MODEL

TM_PROJ = 512
TQ = 256
ROWS_PER_TILE = TQ // GRID_W
KEY_TILES = 3
WIN_STACK = 2
VMEM_LIMIT = 56 * 1024 * 1024


def _const_spec(shape):
    return pl.BlockSpec(shape, lambda *_: (0,) * len(shape))


def _proj_kernel(x_ref, w_ref, bd_ref, gains_ref,
                 qa_ref, ka_ref, va_ref, za_ref, qb_ref, kb_ref, vb_ref, zb_ref, ga_ref, gb_ref):
    x = x_ref[...]
    inv_rms = lax.rsqrt(jnp.mean(x * x, axis=-1, keepdims=True) + EPS)
    xb = x.astype(BF16)

    def proj(lo, width):
        return jnp.dot(xb, w_ref[:, lo:lo + width], preferred_element_type=F32) * inv_rms

    def head_norm(p, gain):
        width = p.shape[-1]
        sq = (p * p).astype(BF16)
        parts = []
        for lo in range(0, width, LANE_GROUP):
            w = min(LANE_GROUP, width - lo)
            parts.append(jnp.dot(sq[:, lo:lo + w], bd_ref[:w, :w], preferred_element_type=F32))
        ssq = parts[0] if len(parts) == 1 else jnp.concatenate(parts, axis=-1)
        return p * lax.rsqrt(ssq * (1.0 / HEAD_DIM) + EPS) * gain

    def silu(z):
        return z * jax.nn.sigmoid(z)

    def spread_kv(kv):
        swapped = pltpu.roll(kv, HEAD_DIM, 1)
        low = lax.broadcasted_iota(jnp.int32, kv.shape, 1) < HEAD_DIM
        first = jnp.where(low, kv, swapped).astype(BF16)
        second = jnp.where(low, swapped, kv).astype(BF16)
        return jnp.concatenate([first, first, second, second], axis=-1)

    qa_ref[...] = head_norm(proj(OFF_QA, W_A), gains_ref[0:1, :]).astype(BF16)
    ka_ref[...] = head_norm(proj(OFF_KA, W_A), gains_ref[1:2, :]).astype(BF16)
    va_ref[...] = proj(OFF_VA, W_A).astype(BF16)
    za_ref[...] = silu(proj(OFF_ZA, W_A)).astype(BF16)
    qb_ref[...] = head_norm(proj(OFF_QB, W_B), gains_ref[2:3, :]).astype(BF16)
    kb_ref[...] = spread_kv(head_norm(proj(OFF_KB, W_KV_B), gains_ref[3:4, :W_KV_B]))
    vb_ref[...] = spread_kv(proj(OFF_VB, W_KV_B))
    zb_ref[...] = silu(proj(OFF_ZB, W_B)).astype(BF16)
    for c in range(0, D_MODEL, W_A):
        ga_ref[:, c:c + W_A] = jax.nn.sigmoid(proj(OFF_GA + c, W_A)).astype(BF16)
        gb_ref[:, c:c + W_A] = jax.nn.sigmoid(proj(OFF_GB + c, W_A)).astype(BF16)


def _project(x2d, w_bf, bd, gains):
    n_tok = x2d.shape[0]
    tm = TM_PROJ
    assert n_tok % tm == 0
    tok = lambda width: pl.BlockSpec((tm, width), lambda i: (i, 0))
    widths = (W_A, W_A, W_A, W_A, W_B, W_B, W_B, W_B, D_MODEL, D_MODEL)
    return pl.pallas_call(
        _proj_kernel,
        grid=(n_tok // tm,),
        in_specs=[tok(D_MODEL), _const_spec(w_bf.shape), _const_spec(bd.shape), _const_spec(gains.shape)],
        out_specs=[tok(w) for w in widths],
        out_shape=[jax.ShapeDtypeStruct((n_tok, w), BF16) for w in widths],
        compiler_params=pltpu.CompilerParams(dimension_semantics=("arbitrary",),
                                             vmem_limit_bytes=VMEM_LIMIT),
        name="proj",
    )(x2d, w_bf, bd, gains)


def _softmax(s, sink=None):
    m = jnp.max(s, axis=-1, keepdims=True)
    if sink is not None:
        m = jnp.maximum(m, sink)
    p = jnp.exp2(s - m)
    l = jnp.sum(p, axis=-1, keepdims=True)
    if sink is not None:
        l = l + jnp.exp2(sink - m)
    return p.astype(BF16), 1.0 / l


def _head_masks():
    lane_head = lax.broadcasted_iota(jnp.int32, (1, LANE_GROUP), 1) // HEAD_DIM
    return [(lane_head == j).astype(F32) for j in range(HEADS_PER_GROUP)]


def _window(prev_ref, cur_ref, next_ref, cols):
    return jnp.concatenate([prev_ref[:, cols], cur_ref[:, cols], next_ref[:, cols]], axis=0)


def _na_branch(q_ref, k_refs, v_refs, z_ref, bias_ref, case, masks):
    groups = []
    for g in range(W_A // LANE_GROUP):
        cols = slice(g * LANE_GROUP, (g + 1) * LANE_GROUP)
        q4 = q_ref[:, cols].astype(F32)
        k4 = _window(*k_refs, cols)
        v4 = _window(*v_refs, cols)
        o4 = jnp.zeros((TQ, LANE_GROUP), F32)
        for j in range(HEADS_PER_GROUP):
            qm = (q4 * masks[j]).astype(BF16)
            s = lax.dot_general(qm, k4, (((1,), (1,)), ((), ())), preferred_element_type=F32)
            p, linv = _softmax(s + bias_ref[case, g * HEADS_PER_GROUP + j])
            pv = jnp.dot(p, v4, preferred_element_type=F32)
            o4 = o4 + pv * (linv * masks[j])
        groups.append((o4 * z_ref[:, cols].astype(F32)).astype(BF16))
    return jnp.concatenate(groups, axis=-1)


def _win_branch(q_ref, k_refs, v_refs, z_ref, bias_ref, sink_ref, t, nt, masks):
    n_sub = TQ // BLOCK
    groups = []
    for g in range(KVH_B):
        cols = slice(g * LANE_GROUP, (g + 1) * LANE_GROUP)
        kwin = _window(*k_refs, cols)
        vwin = _window(*v_refs, cols)
        subs = []
        for sub in range(n_sub):
            rows = slice(sub * BLOCK, (sub + 1) * BLOCK)
            key_lo = TQ + (sub - 1) * BLOCK
            k3 = kwin[key_lo:key_lo + 3 * BLOCK]
            v3 = vwin[key_lo:key_lo + 3 * BLOCK]
            if sub == 0:
                case = jnp.where(t == 0, 0, 1)
            elif sub == n_sub - 1:
                case = jnp.where(t == nt - 1, 2, 1)
            else:
                case = 1
            q4 = q_ref[rows, cols].astype(F32)
            o4 = jnp.zeros((BLOCK, LANE_GROUP), F32)
            for j0 in range(0, G_B, WIN_STACK):
                heads = range(j0, j0 + WIN_STACK)
                qcat = jnp.concatenate([(q4 * masks[j]).astype(BF16) for j in heads], axis=0)
                s = lax.dot_general(qcat, k3, (((1,), (1,)), ((), ())), preferred_element_type=F32)
                s = s + bias_ref[case, g, j0 * BLOCK:(j0 + WIN_STACK) * BLOCK]
                parts = [_softmax(s[i * BLOCK:(i + 1) * BLOCK], sink_ref[g * G_B + j])
                         for i, j in enumerate(heads)]
                pv = jnp.dot(jnp.concatenate([p for p, _ in parts], axis=0), v3, preferred_element_type=F32)
                for i, j in enumerate(heads):
                    o4 = o4 + pv[i * BLOCK:(i + 1) * BLOCK] * (parts[i][1] * masks[j])
            subs.append((o4 * z_ref[rows, cols].astype(F32)).astype(BF16))
        groups.append(jnp.concatenate(subs, axis=0))
    return jnp.concatenate(groups, axis=-1)


def _attn_kernel(x_ref, qa_ref, kap_ref, kac_ref, kan_ref, vap_ref, vac_ref, van_ref, za_ref,
                 qb_ref, kbp_ref, kbc_ref, kbn_ref, vbp_ref, vbc_ref, vbn_ref, zb_ref,
                 ga_ref, gb_ref, bias_a_ref, bias_b_ref, sink_ref, woa_ref, wob_ref, wout_ref,
                 y_ref, *, nt):
    t = pl.program_id(1)
    case = jnp.where(t == 0, 0, jnp.where(t == nt - 1, 2, 1))
    masks = _head_masks()
    oa = _na_branch(qa_ref, (kap_ref, kac_ref, kan_ref), (vap_ref, vac_ref, van_ref), za_ref,
                    bias_a_ref, case, masks)
    ob = _win_branch(qb_ref, (kbp_ref, kbc_ref, kbn_ref), (vbp_ref, vbc_ref, vbn_ref), zb_ref,
                     bias_b_ref, sink_ref, t, nt, masks)
    a = jnp.dot(oa, woa_ref[...], preferred_element_type=F32)
    b = jnp.dot(ob, wob_ref[...], preferred_element_type=F32)
    merged = ga_ref[...].astype(F32) * a + gb_ref[...].astype(F32) * b
    y_ref[...] = x_ref[...] + jnp.dot(merged.astype(BF16), wout_ref[...], preferred_element_type=F32)


def _attend(x2d, proj, bias_a, bias_b, sink, woa, wob, wout, batch, seq):
    qa, ka, va, za, qb, kb, vb, zb, ga, gb = proj
    nt = seq // TQ
    assert seq % TQ == 0 and nt >= 3
    cur = lambda width: pl.BlockSpec((TQ, width), lambda b, t: (b * nt + t, 0))
    halo = [pl.BlockSpec((TQ, W_A), lambda b, t: (b * nt + jnp.maximum(t - 1, 0), 0)),
            cur(W_A),
            pl.BlockSpec((TQ, W_A), lambda b, t: (b * nt + jnp.minimum(t + 1, nt - 1), 0))]
    smem = pl.BlockSpec(memory_space=pltpu.SMEM)
    in_specs = ([cur(D_MODEL), cur(W_A)] + halo + halo + [cur(W_A), cur(W_B)] + halo + halo
                + [cur(W_B), cur(D_MODEL), cur(D_MODEL), _const_spec(bias_a.shape), _const_spec(bias_b.shape),
                   smem, _const_spec(woa.shape), _const_spec(wob.shape), _const_spec(wout.shape)])
    return pl.pallas_call(
        functools.partial(_attn_kernel, nt=nt),
        grid=(batch, nt),
        in_specs=in_specs,
        out_specs=cur(D_MODEL),
        out_shape=jax.ShapeDtypeStruct((batch * seq, D_MODEL), F32),
        compiler_params=pltpu.CompilerParams(dimension_semantics=("arbitrary", "arbitrary"),
                                             vmem_limit_bytes=VMEM_LIMIT),
        name="attn_out",
    )(x2d, qa, ka, ka, ka, va, va, va, za, qb, kb, kb, kb, vb, vb, vb, zb, ga, gb,
      bias_a, bias_b, sink, woa, wob, wout)


def _toeplitz(v, m, n):
    w = v.shape[-1]
    assert n <= w - 1
    flat = jnp.tile(v, m)[..., :m * (w - 1)]
    return flat.reshape(v.shape[:-1] + (m, w - 1))[..., :n]


def _na_bias(rpb, rows):
    kr = min(NA_ROWS, rows)
    assert kr == NA_ROWS and rows >= 3 * ROWS_PER_TILE
    n_key_rows = KEY_TILES * ROWS_PER_TILE
    rpb = rpb.astype(F32) * LOG2E
    ring_pad = jnp.zeros(rpb.shape[:-1] + (2 * GRID_W - (2 * NA_COLS - 1),), F32)
    ring = jnp.concatenate([rpb[..., NA_COLS - 1:], ring_pad, rpb[..., :NA_COLS - 1]], axis=-1)
    col = _toeplitz(ring, GRID_W, GRID_W)
    first = NA_ROWS - 1 - ROWS_PER_TILE
    per_row = [col[:, first - i:first - i + n_key_rows] for i in range(ROWS_PER_TILE)]
    vals = jnp.transpose(jnp.stack(per_row, axis=1), (0, 1, 3, 2, 4)).reshape(H_A, TQ, KEY_TILES * TQ)
    i = np.arange(ROWS_PER_TILE)[:, None, None, None]
    qc = np.arange(GRID_W)[None, :, None, None]
    j = np.arange(n_key_rows)[None, None, :, None]
    kc = np.arange(GRID_W)[None, None, None, :]
    cs = np.clip(qc - NA_COLS // 2, 0, GRID_W - NA_COLS)
    col_ok = (kc >= cs) & (kc < cs + NA_COLS)
    shape = (ROWS_PER_TILE, GRID_W, n_key_rows, GRID_W)
    row_ok = [
        (j >= ROWS_PER_TILE) & (j < ROWS_PER_TILE + kr),
        (j - i >= 0) & (j - i < kr),
        (j >= 0) & (j < kr),
    ]
    out = []
    for ok in row_ok:
        valid = np.broadcast_to(ok & col_ok, shape).reshape(TQ, KEY_TILES * TQ)
        out.append(jnp.where(valid[None], vals, NEG))
    return jnp.stack(out)


def _t5_bucket(rel):
    half = T5_BUCKETS // 2
    max_exact = half // 2
    n = jnp.abs(rel)
    large = max_exact + (jnp.log(jnp.maximum(n, 1).astype(jnp.float32) / max_exact)
                         / math.log(T5_MAX_DIST / max_exact) * (half - max_exact)).astype(jnp.int32)
    large = jnp.minimum(large, half - 1)
    return jnp.where(rel > 0, half, 0) + jnp.where(n < max_exact, n, large)


def _win_bias(t5_table):
    ring = 4 * BLOCK
    u = jnp.arange(ring)
    rel = jnp.where(u < ring // 2, u, u - ring)
    onehot = (_t5_bucket(rel)[:, None] == jnp.arange(T5_BUCKETS)[None, :]).astype(F32)
    by_rel = jnp.dot(onehot, t5_table.astype(F32) * LOG2E, precision=lax.Precision.HIGHEST)
    by_rel = jnp.where((jnp.abs(rel) <= WINDOW)[:, None], by_rel, NEG).T
    bias = _toeplitz(jnp.roll(by_rel, BLOCK, axis=-1), BLOCK, 3 * BLOCK)
    blk = np.arange(3 * BLOCK) // BLOCK
    cases = [jnp.where((blk != 0)[None, None], bias, NEG), bias, jnp.where((blk != 2)[None, None], bias, NEG)]
    return jnp.stack([c.reshape(KVH_B, G_B * BLOCK, 3 * BLOCK) for c in cases])


def _layer_constants(qn_a, kn_a, qn_b, kn_b):
    q_scale = LOG2E / math.sqrt(HEAD_DIM)
    tile = lambda gvec, s: jnp.tile(gvec.astype(F32) * s, W_A // HEAD_DIM)
    gains = jnp.stack([tile(qn_a, q_scale), tile(kn_a, 1.0), tile(qn_b, q_scale), tile(kn_b, 1.0)])
    lane = np.arange(LANE_GROUP)
    bd = jnp.asarray(lane[:, None] // HEAD_DIM == lane[None, :] // HEAD_DIM, BF16)
    return gains, bd


def _encoder_layer(x, norm_g, w_in, qn_a, kn_a, rpb_a, qn_b, kn_b, sink_b, w_o_a, w_o_b, w_out, t5_table):
    batch, seq, d = x.shape
    assert d == D_MODEL and seq % GRID_W == 0
    x2d = x.reshape(batch * seq, d)
    gains, bd = _layer_constants(qn_a, kn_a, qn_b, kn_b)
    w_bf = (norm_g.astype(F32)[:, None] * w_in).astype(BF16)
    proj = _project(x2d, w_bf, bd, gains)
    y = _attend(x2d, proj, _na_bias(rpb_a, seq // GRID_W), _win_bias(t5_table), sink_b.astype(F32) * LOG2E,
                w_o_a.astype(BF16), w_o_b.astype(BF16), w_out.astype(BF16), batch, seq)
    return y.reshape(batch, seq, d)


def kernel(x_prompt, x_sample, norm_g, w_in, qn_a, kn_a, rpb_a, qn_b, kn_b, sink_b, w_o_a, w_o_b, w_out, t5_table):
    y_prompt, y_sample = x_prompt, x_sample
    for l in range(norm_g.shape[0]):
        params = (norm_g[l], w_in[l], qn_a[l], kn_a[l], rpb_a[l], qn_b[l], kn_b[l], sink_b[l],
                  w_o_a[l], w_o_b[l], w_out[l], t5_table)
        y_prompt = _encoder_layer(y_prompt, *params)
        y_sample = _encoder_layer(y_sample, *params)
    return (y_prompt, y_sample)
```

```python
import functools
import math

import jax
import jax.numpy as jnp
import numpy as np
from jax import lax
from jax.experimental import pallas as pl
from jax.experimental.pallas import tpu as pltpu

F32 = jnp.float32
BF16 = jnp.bfloat16

D_MODEL = 1024
HEAD_DIM = 64
H_A = 8
H_B = 8
KVH_B = 2
G_B = H_B // KVH_B
W_A = H_A * HEAD_DIM
W_B = H_B * HEAD_DIM
W_KV_B = KVH_B * HEAD_DIM
GRID_W = 64
NA_ROWS = 8
NA_COLS = 16
WINDOW = 128
BLOCK = 128
T5_BUCKETS = 32
T5_MAX_DIST = 128
EPS = 1e-6
NEG = -1e30
LOG2E = math.log2(math.e)

SUBLANES = 8
LANE_GROUP = 256
HEADS_PER_GROUP = LANE_GROUP // HEAD_DIM
OFF_QA, OFF_KA, OFF_VA, OFF_ZA = 0, W_A, 2 * W_A, 3 * W_A
OFF_QB = 4 * W_A
OFF_KB = OFF_QB + W_B
OFF_VB = OFF_KB + W_KV_B
OFF_ZB = OFF_VB + W_KV_B
OFF_GA = OFF_ZB + W_B
OFF_GB = OFF_GA + D_MODEL
D_IN = OFF_GB + D_MODEL

TM_PROJ = 512
TQ = 256
ROWS_PER_TILE = TQ // GRID_W
KEY_TILES = 3
WIN_STACK = 2
NA_QK_ROWS = 256
WIN_QK_ROWS = 128
LOOKAHEAD = 1
VMEM_LIMIT = 56 * 1024 * 1024


def _const_spec(shape):
    return pl.BlockSpec(shape, lambda *_: (0,) * len(shape))


def _proj_kernel(x_ref, w_ref, bd_ref, gains_ref,
                 qa_ref, ka_ref, va_ref, za_ref, qb_ref, kb_ref, vb_ref, zb_ref, ga_ref, gb_ref):
    x = x_ref[...]
    inv_rms = lax.rsqrt(jnp.mean(x * x, axis=-1, keepdims=True) + EPS)
    xb = x.astype(BF16)

    def proj(lo, width):
        return jnp.dot(xb, w_ref[:, lo:lo + width], preferred_element_type=F32) * inv_rms

    def head_norm(p, gain):
        width = p.shape[-1]
        sq = (p * p).astype(BF16)
        parts = []
        for lo in range(0, width, LANE_GROUP):
            w = min(LANE_GROUP, width - lo)
            parts.append(jnp.dot(sq[:, lo:lo + w], bd_ref[:w, :w], preferred_element_type=F32))
        ssq = parts[0] if len(parts) == 1 else jnp.concatenate(parts, axis=-1)
        return p * lax.rsqrt(ssq * (1.0 / HEAD_DIM) + EPS) * gain

    def silu(z):
        return z * jax.nn.sigmoid(z)

    def spread_kv(kv):
        swapped = pltpu.roll(kv, HEAD_DIM, 1)
        low = lax.broadcasted_iota(jnp.int32, kv.shape, 1) < HEAD_DIM
        first = jnp.where(low, kv, swapped).astype(BF16)
        second = jnp.where(low, swapped, kv).astype(BF16)
        return jnp.concatenate([first, first, second, second], axis=-1)

    qa_ref[...] = head_norm(proj(OFF_QA, W_A), gains_ref[0:1, :]).T.astype(BF16)
    ka_ref[...] = head_norm(proj(OFF_KA, W_A), gains_ref[1:2, :]).astype(BF16)
    va_ref[...] = proj(OFF_VA, W_A).T.astype(BF16)
    za_ref[...] = silu(proj(OFF_ZA, W_A)).astype(BF16)
    qb_ref[...] = head_norm(proj(OFF_QB, W_B), gains_ref[2:3, :]).T.astype(BF16)
    kb_ref[...] = spread_kv(head_norm(proj(OFF_KB, W_KV_B), gains_ref[3:4, :W_KV_B]))
    vb_ref[...] = proj(OFF_VB, W_KV_B).T.astype(BF16)
    zb_ref[...] = silu(proj(OFF_ZB, W_B)).astype(BF16)
    for c in range(0, D_MODEL, W_A):
        ga_ref[:, c:c + W_A] = jax.nn.sigmoid(proj(OFF_GA + c, W_A)).astype(BF16)
        gb_ref[:, c:c + W_A] = jax.nn.sigmoid(proj(OFF_GB + c, W_A)).astype(BF16)


def _project(x2d, w_bf, bd, gains):
    n_tok = x2d.shape[0]
    tm = TM_PROJ
    assert n_tok % tm == 0
    tok = lambda width: pl.BlockSpec((tm, width), lambda i: (i, 0))
    tok_t = lambda height: pl.BlockSpec((height, tm), lambda i: (0, i))
    widths = (None, W_A, None, W_A, None, W_B, None, W_B, D_MODEL, D_MODEL)
    transposed = {0: W_A, 2: W_A, 4: W_B, 6: W_KV_B}
    out_specs = [tok_t(transposed[i]) if w is None else tok(w) for i, w in enumerate(widths)]
    out_shape = [jax.ShapeDtypeStruct((transposed[i], n_tok) if w is None else (n_tok, w), BF16)
                 for i, w in enumerate(widths)]
    return pl.pallas_call(
        _proj_kernel,
        grid=(n_tok // tm,),
        in_specs=[tok(D_MODEL), _const_spec(w_bf.shape), _const_spec(bd.shape), _const_spec(gains.shape)],
        out_specs=out_specs,
        out_shape=out_shape,
        compiler_params=pltpu.CompilerParams(dimension_semantics=("arbitrary",),
                                             vmem_limit_bytes=VMEM_LIMIT),
        name="proj",
    )(x2d, w_bf, bd, gains)


def _reduce_keys(x, combine, reduce):
    while x.shape[0] % (2 * SUBLANES) == 0:
        half = x.shape[0] // 2
        x = combine(x[:half], x[half:])
    return reduce(x, axis=0, keepdims=True)


def _softmax_t(s_t, sink=None):
    m = _reduce_keys(s_t, jnp.maximum, jnp.max)
    if sink is not None:
        m = jnp.maximum(m, sink)
    p = jnp.exp2(s_t - m)
    l = _reduce_keys(p, jnp.add, jnp.sum)
    if sink is not None:
        l = l + jnp.exp2(sink - m)
    return p.astype(BF16), 1.0 / l


def _head_rows(qt_ref, g, j, queries):
    lo = g * LANE_GROUP + j * HEAD_DIM
    head = qt_ref[lo:lo + HEAD_DIM, queries]
    above = jnp.zeros((j * HEAD_DIM, head.shape[1]), BF16)
    below = jnp.zeros(((HEADS_PER_GROUP - 1 - j) * HEAD_DIM, head.shape[1]), BF16)
    return jnp.concatenate([part for part in (above, head, below) if part.shape[0]], axis=0)


def _qk_t(keys, qt_masked, chunk):
    parts = [jnp.dot(keys[r:r + chunk], qt_masked, preferred_element_type=F32)
             for r in range(0, keys.shape[0], chunk)]
    return jnp.concatenate(parts, axis=0)


def _na_blocks(qt_ref, k_refs, vt_refs, bias_ref, case):
    blocks = []
    for g in range(W_A // LANE_GROUP):
        cols = slice(g * LANE_GROUP, (g + 1) * LANE_GROUP)
        for j in range(HEADS_PER_GROUP):
            h = g * HEADS_PER_GROUP + j

            def logits(g=g, cols=cols, j=j):
                k4 = jnp.concatenate([r[:, cols] for r in k_refs], axis=0)
                return _qk_t(k4, _head_rows(qt_ref, g, j, slice(None)), NA_QK_ROWS)

            def finish(s_t, h=h):
                feat = slice(h * HEAD_DIM, (h + 1) * HEAD_DIM)
                v_t = jnp.concatenate([r[feat, :] for r in vt_refs], axis=1)
                p_t, linv = _softmax_t(s_t + bias_ref[case, h])
                return jnp.dot(v_t, p_t, preferred_element_type=F32) * linv

            blocks.append((logits, finish))
    return blocks


def _win_blocks(qt_ref, k_refs, vt_refs, bias_ref, sink_ref, t, nt):
    n_sub = TQ // BLOCK
    blocks = []
    for g in range(KVH_B):
        cols = slice(g * LANE_GROUP, (g + 1) * LANE_GROUP)
        for sub in range(n_sub):
            rows = slice(sub * BLOCK, (sub + 1) * BLOCK)
            key_lo = TQ + (sub - 1) * BLOCK
            if sub == 0:
                case = jnp.where(t == 0, 0, 1)
            elif sub == n_sub - 1:
                case = jnp.where(t == nt - 1, 2, 1)
            else:
                case = 1
            for pair in range(G_B // WIN_STACK):

                def logits(g=g, cols=cols, rows=rows, key_lo=key_lo, pair=pair):
                    kwin = jnp.concatenate([r[:, cols] for r in k_refs], axis=0)
                    heads = range(pair * WIN_STACK, (pair + 1) * WIN_STACK)
                    qt_pair = jnp.concatenate([_head_rows(qt_ref, g, j, rows) for j in heads], axis=1)
                    return _qk_t(kwin[key_lo:key_lo + 3 * BLOCK], qt_pair, WIN_QK_ROWS)

                def finish(s_t, g=g, key_lo=key_lo, pair=pair, case=case):
                    vwin_t = jnp.concatenate([r[g * HEAD_DIM:(g + 1) * HEAD_DIM, :] for r in vt_refs], axis=1)
                    p_t, linv = _softmax_t(s_t + bias_ref[case, g, pair],
                                           sink_ref[g * (G_B // WIN_STACK) + pair])
                    v3_t = vwin_t[:, key_lo:key_lo + 3 * BLOCK]
                    return jnp.dot(v3_t, p_t, preferred_element_type=F32) * linv

                blocks.append((logits, finish))
    return blocks


def _run_pipelined(blocks):
    outs, pending = [], []
    for logits, finish in blocks:
        pending.append((logits(), finish))
        if len(pending) > LOOKAHEAD:
            s_t, fin = pending.pop(0)
            outs.append(fin(s_t))
    outs += [fin(s_t) for s_t, fin in pending]
    return outs


def _assemble_win(outs):
    n_sub, n_pair = TQ // BLOCK, G_B // WIN_STACK
    feats = []
    for g in range(KVH_B):
        per_sub = []
        for sub in range(n_sub):
            pairs = outs[(g * n_sub + sub) * n_pair:(g * n_sub + sub + 1) * n_pair]
            heads = [o[:, i * BLOCK:(i + 1) * BLOCK] for o in pairs for i in range(WIN_STACK)]
            per_sub.append(jnp.concatenate(heads, axis=0))
        feats.append(jnp.concatenate(per_sub, axis=1))
    return jnp.concatenate(feats, axis=0)


def _attn_kernel(x_ref, qa_ref, kap_ref, kac_ref, kan_ref, vap_ref, vac_ref, van_ref, za_ref,
                 qb_ref, kbp_ref, kbc_ref, kbn_ref, vbp_ref, vbc_ref, vbn_ref, zb_ref,
                 ga_ref, gb_ref, bias_a_ref, bias_b_ref, sink_ref, woa_ref, wob_ref, wout_ref,
                 y_ref, *, nt):
    t = pl.program_id(1)
    case = jnp.where(t == 0, 0, jnp.where(t == nt - 1, 2, 1))
    na = _na_blocks(qa_ref, (kap_ref, kac_ref, kan_ref), (vap_ref, vac_ref, van_ref), bias_a_ref, case)
    win = _win_blocks(qb_ref, (kbp_ref, kbc_ref, kbn_ref), (vbp_ref, vbc_ref, vbn_ref), bias_b_ref, sink_ref, t, nt)
    outs = _run_pipelined(na + win)
    oa = (jnp.concatenate(outs[:len(na)], axis=0).T * za_ref[...].astype(F32)).astype(BF16)
    ob = (_assemble_win(outs[len(na):]).T * zb_ref[...].astype(F32)).astype(BF16)
    a = jnp.dot(oa, woa_ref[...], preferred_element_type=F32)
    b = jnp.dot(ob, wob_ref[...], preferred_element_type=F32)
    merged = ga_ref[...].astype(F32) * a + gb_ref[...].astype(F32) * b
    y_ref[...] = x_ref[...] + jnp.dot(merged.astype(BF16), wout_ref[...], preferred_element_type=F32)


def _attend(x2d, proj, bias_a, bias_b, sink, woa, wob, wout, batch, seq):
    qa_t, ka, va_t, za, qb_t, kb, vb_t, zb, ga, gb = proj
    nt = seq // TQ
    assert seq % TQ == 0 and nt >= 3
    prev_tile = lambda b, t: b * nt + jnp.maximum(t - 1, 0)
    cur_tile = lambda b, t: b * nt + t
    next_tile = lambda b, t: b * nt + jnp.minimum(t + 1, nt - 1)
    tiles = (prev_tile, cur_tile, next_tile)
    rows = lambda width, tile=cur_tile: pl.BlockSpec((TQ, width), lambda b, t: (tile(b, t), 0))
    cols = lambda height, tile: pl.BlockSpec((height, TQ), lambda b, t: (0, tile(b, t)))
    in_specs = ([rows(D_MODEL), cols(W_A, cur_tile)] + [rows(W_A, f) for f in tiles] + [cols(W_A, f) for f in tiles]
                + [rows(W_A), cols(W_B, cur_tile)] + [rows(W_B, f) for f in tiles] + [cols(W_KV_B, f) for f in tiles]
                + [rows(W_B), rows(D_MODEL), rows(D_MODEL), _const_spec(bias_a.shape), _const_spec(bias_b.shape),
                   _const_spec(sink.shape), _const_spec(woa.shape), _const_spec(wob.shape),
                   _const_spec(wout.shape)])
    return pl.pallas_call(
        functools.partial(_attn_kernel, nt=nt),
        grid=(batch, nt),
        in_specs=in_specs,
        out_specs=rows(D_MODEL),
        out_shape=jax.ShapeDtypeStruct((batch * seq, D_MODEL), F32),
        compiler_params=pltpu.CompilerParams(dimension_semantics=("arbitrary", "arbitrary"),
                                             vmem_limit_bytes=VMEM_LIMIT),
        name="attn_out",
    )(x2d, qa_t, ka, ka, ka, va_t, va_t, va_t, za, qb_t, kb, kb, kb, vb_t, vb_t, vb_t, zb, ga, gb,
      bias_a, bias_b, sink, woa, wob, wout)


def _toeplitz(v, m, n):
    w = v.shape[-1]
    assert n <= w - 1
    flat = jnp.tile(v, m)[..., :m * (w - 1)]
    return flat.reshape(v.shape[:-1] + (m, w - 1))[..., :n]


def _na_bias(rpb, rows):
    kr = min(NA_ROWS, rows)
    assert kr == NA_ROWS and rows >= 3 * ROWS_PER_TILE
    n_key_rows = KEY_TILES * ROWS_PER_TILE
    rpb = rpb.astype(F32) * LOG2E
    ring_pad = jnp.zeros(rpb.shape[:-1] + (2 * GRID_W - (2 * NA_COLS - 1),), F32)
    ring = jnp.concatenate([rpb[..., NA_COLS - 1:], ring_pad, rpb[..., :NA_COLS - 1]], axis=-1)
    col = _toeplitz(ring, GRID_W, GRID_W)
    first = NA_ROWS - 1 - ROWS_PER_TILE
    per_row = [col[:, first - i:first - i + n_key_rows] for i in range(ROWS_PER_TILE)]
    vals = jnp.transpose(jnp.stack(per_row, axis=1), (0, 1, 3, 2, 4)).reshape(H_A, TQ, KEY_TILES * TQ)
    i = np.arange(ROWS_PER_TILE)[:, None, None, None]
    qc = np.arange(GRID_W)[None, :, None, None]
    j = np.arange(n_key_rows)[None, None, :, None]
    kc = np.arange(GRID_W)[None, None, None, :]
    cs = np.clip(qc - NA_COLS // 2, 0, GRID_W - NA_COLS)
    col_ok = (kc >= cs) & (kc < cs + NA_COLS)
    shape = (ROWS_PER_TILE, GRID_W, n_key_rows, GRID_W)
    row_ok = [
        (j >= ROWS_PER_TILE) & (j < ROWS_PER_TILE + kr),
        (j - i >= 0) & (j - i < kr),
        (j >= 0) & (j < kr),
    ]
    out = []
    for ok in row_ok:
        valid = np.broadcast_to(ok & col_ok, shape).reshape(TQ, KEY_TILES * TQ)
        out.append(jnp.swapaxes(jnp.where(valid[None], vals, NEG), -1, -2))
    return jnp.stack(out)


def _t5_bucket(rel):
    half = T5_BUCKETS // 2
    max_exact = half // 2
    n = jnp.abs(rel)
    large = max_exact + (jnp.log(jnp.maximum(n, 1).astype(jnp.float32) / max_exact)
                         / math.log(T5_MAX_DIST / max_exact) * (half - max_exact)).astype(jnp.int32)
    large = jnp.minimum(large, half - 1)
    return jnp.where(rel > 0, half, 0) + jnp.where(n < max_exact, n, large)


def _win_bias(t5_table):
    ring = 4 * BLOCK
    u = jnp.arange(ring)
    rel = jnp.where(u < ring // 2, u, u - ring)
    onehot = (_t5_bucket(rel)[:, None] == jnp.arange(T5_BUCKETS)[None, :]).astype(F32)
    by_rel = jnp.dot(onehot, t5_table.astype(F32) * LOG2E, precision=lax.Precision.HIGHEST)
    by_rel = jnp.where((jnp.abs(rel) <= WINDOW)[:, None], by_rel, NEG).T
    bias = _toeplitz(jnp.roll(by_rel, BLOCK, axis=-1), BLOCK, 3 * BLOCK)
    blk = np.arange(3 * BLOCK) // BLOCK
    cases = [jnp.where((blk != 0)[None, None], bias, NEG), bias, jnp.where((blk != 2)[None, None], bias, NEG)]
    shape = (KVH_B, G_B // WIN_STACK, WIN_STACK * BLOCK, 3 * BLOCK)
    return jnp.stack([jnp.swapaxes(c.reshape(shape), -1, -2) for c in cases])


def _layer_constants(qn_a, kn_a, qn_b, kn_b):
    q_scale = LOG2E / math.sqrt(HEAD_DIM)
    tile = lambda gvec, s: jnp.tile(gvec.astype(F32) * s, W_A // HEAD_DIM)
    gains = jnp.stack([tile(qn_a, q_scale), tile(kn_a, 1.0), tile(qn_b, q_scale), tile(kn_b, 1.0)])
    lane = np.arange(LANE_GROUP)
    bd = jnp.asarray(lane[:, None] // HEAD_DIM == lane[None, :] // HEAD_DIM, BF16)
    return gains, bd


def _encoder_layer(x, norm_g, w_in, qn_a, kn_a, rpb_a, qn_b, kn_b, sink_b, w_o_a, w_o_b, w_out, t5_table):
    batch, seq, d = x.shape
    assert d == D_MODEL and seq % GRID_W == 0
    x2d = x.reshape(batch * seq, d)
    gains, bd = _layer_constants(qn_a, kn_a, qn_b, kn_b)
    w_bf = (norm_g.astype(F32)[:, None] * w_in).astype(BF16)
    proj = _project(x2d, w_bf, bd, gains)
    sink = jnp.repeat(sink_b.astype(F32) * LOG2E, BLOCK).reshape(H_B // WIN_STACK, 1, WIN_STACK * BLOCK)
    y = _attend(x2d, proj, _na_bias(rpb_a, seq // GRID_W), _win_bias(t5_table), sink,
                w_o_a.astype(BF16), w_o_b.astype(BF16), w_out.astype(BF16), batch, seq)
    return y.reshape(batch, seq, d)


def kernel(x_prompt, x_sample, norm_g, w_in, qn_a, kn_a, rpb_a, qn_b, kn_b, sink_b, w_o_a, w_o_b, w_out, t5_table):
    y_prompt, y_sample = x_prompt, x_sample
    for l in range(norm_g.shape[0]):
        params = (norm_g[l], w_in[l], qn_a[l], kn_a[l], rpb_a[l], qn_b[l], kn_b[l], sink_b[l],
                  w_o_a[l], w_o_b[l], w_out[l], t5_table)
        y_prompt = _encoder_layer(y_prompt, *params)
        y_sample = _encoder_layer(y_sample, *params)
    return (y_prompt, y_sample)
```

```python
import functools
import math

import jax
import jax.numpy as jnp
import numpy as np
from jax import lax
from jax.experimental import pallas as pl
from jax.experimental.pallas import tpu as pltpu

F32 = jnp.float32
BF16 = jnp.bfloat16

D_MODEL = 1024
HEAD_DIM = 64
H_A = 8
H_B = 8
KVH_B = 2
G_B = H_B // KVH_B
W_A = H_A * HEAD_DIM
W_B = H_B * HEAD_DIM
W_KV_B = KVH_B * HEAD_DIM
GRID_W = 64
NA_ROWS = 8
NA_COLS = 16
WINDOW = 128
BLOCK = 128
T5_BUCKETS = 32
T5_MAX_DIST = 128
EPS = 1e-6
NEG = -1e30
LOG2E = math.log2(math.e)

SUBLANES = 8
LANE_GROUP = 256
HEADS_PER_GROUP = LANE_GROUP // HEAD_DIM
OFF_QA, OFF_KA, OFF_VA, OFF_ZA = 0, W_A, 2 * W_A, 3 * W_A
OFF_QB = 4 * W_A
OFF_KB = OFF_QB + W_B
OFF_VB = OFF_KB + W_KV_B
OFF_ZB = OFF_VB + W_KV_B
OFF_GA = OFF_ZB + W_B
OFF_GB = OFF_GA + D_MODEL
D_IN = OFF_GB + D_MODEL

TQ = 256
ROWS_PER_TILE = TQ // GRID_W
KEY_TILES = 3
Q_SLOTS = 2
WIN_STACK = 2
NA_QK_ROWS = 256
WIN_QK_ROWS = 128
LOOKAHEAD = 1
VMEM_LIMIT = 56 * 1024 * 1024


def _const_spec(shape):
    return pl.BlockSpec(shape, lambda *_: (0,) * len(shape))


def _project_tile(x_ref, w_ref, bd_ref, gains_ref,
                  ka_ref, vat_ref, kb_ref, vbt_ref, qat_ref, za_ref, qbt_ref, zb_ref, ga_ref, gb_ref):
    x = x_ref[...]
    inv_rms = lax.rsqrt(jnp.mean(x * x, axis=-1, keepdims=True) + EPS)
    xb = x.astype(BF16)

    def proj(lo, width):
        return jnp.dot(xb, w_ref[:, lo:lo + width], preferred_element_type=F32) * inv_rms

    def head_norm(p, gain):
        width = p.shape[-1]
        sq = (p * p).astype(BF16)
        parts = []
        for lo in range(0, width, LANE_GROUP):
            w = min(LANE_GROUP, width - lo)
            parts.append(jnp.dot(sq[:, lo:lo + w], bd_ref[:w, :w], preferred_element_type=F32))
        ssq = parts[0] if len(parts) == 1 else jnp.concatenate(parts, axis=-1)
        return p * lax.rsqrt(ssq * (1.0 / HEAD_DIM) + EPS) * gain

    def silu(z):
        return z * jax.nn.sigmoid(z)

    def spread_kv(kv):
        swapped = pltpu.roll(kv, HEAD_DIM, 1)
        low = lax.broadcasted_iota(jnp.int32, kv.shape, 1) < HEAD_DIM
        first = jnp.where(low, kv, swapped).astype(BF16)
        second = jnp.where(low, swapped, kv).astype(BF16)
        return jnp.concatenate([first, first, second, second], axis=-1)

    def store(ref, value, c=None):
        if c is None:
            ref[...] = value().astype(BF16)
        else:
            ref[:, c:c + W_A] = value().astype(BF16)

    early = [
        lambda: store(ka_ref, lambda: head_norm(proj(OFF_KA, W_A), gains_ref[1:2, :])),
        lambda: store(vat_ref, lambda: proj(OFF_VA, W_A).T),
        lambda: store(kb_ref, lambda: spread_kv(head_norm(proj(OFF_KB, W_KV_B), gains_ref[3:4, :W_KV_B]))),
        lambda: store(vbt_ref, lambda: proj(OFF_VB, W_KV_B).T),
        lambda: store(qat_ref, lambda: head_norm(proj(OFF_QA, W_A), gains_ref[0:1, :]).T),
        lambda: store(qbt_ref, lambda: head_norm(proj(OFF_QB, W_B), gains_ref[2:3, :]).T),
    ]
    late = [
        lambda: store(za_ref, lambda: silu(proj(OFF_ZA, W_A))),
        lambda: store(zb_ref, lambda: silu(proj(OFF_ZB, W_B))),
    ]
    for c in range(0, D_MODEL, W_A):
        late.append(lambda c=c: store(ga_ref, lambda: jax.nn.sigmoid(proj(OFF_GA + c, W_A)), c))
        late.append(lambda c=c: store(gb_ref, lambda: jax.nn.sigmoid(proj(OFF_GB + c, W_A)), c))
    return early, late


def _reduce_keys(x, combine, reduce):
    while x.shape[0] % (2 * SUBLANES) == 0:
        half = x.shape[0] // 2
        x = combine(x[:half], x[half:])
    return reduce(x, axis=0, keepdims=True)


def _softmax_t(s_t, sink=None):
    m = _reduce_keys(s_t, jnp.maximum, jnp.max)
    if sink is not None:
        m = jnp.maximum(m, sink)
    p = jnp.exp2(s_t - m)
    l = _reduce_keys(p, jnp.add, jnp.sum)
    if sink is not None:
        l = l + jnp.exp2(sink - m)
    return p.astype(BF16), 1.0 / l


def _head_rows(qt_ref, g, j, queries):
    lo = g * LANE_GROUP + j * HEAD_DIM
    head = qt_ref[lo:lo + HEAD_DIM, queries]
    above = jnp.zeros((j * HEAD_DIM, head.shape[1]), BF16)
    below = jnp.zeros(((HEADS_PER_GROUP - 1 - j) * HEAD_DIM, head.shape[1]), BF16)
    return jnp.concatenate([part for part in (above, head, below) if part.shape[0]], axis=0)


def _qk_t(keys, qt_masked, chunk):
    parts = [jnp.dot(keys[r:r + chunk], qt_masked, preferred_element_type=F32)
             for r in range(0, keys.shape[0], chunk)]
    return jnp.concatenate(parts, axis=0)


def _na_blocks(qt_ref, k_refs, vt_refs, bias_ref, case):
    blocks = []
    for g in range(W_A // LANE_GROUP):
        cols = slice(g * LANE_GROUP, (g + 1) * LANE_GROUP)
        for j in range(HEADS_PER_GROUP):
            h = g * HEADS_PER_GROUP + j

            def logits(g=g, cols=cols, j=j):
                k4 = jnp.concatenate([r[:, cols] for r in k_refs], axis=0)
                return _qk_t(k4, _head_rows(qt_ref, g, j, slice(None)), NA_QK_ROWS)

            def finish(s_t, h=h):
                feat = slice(h * HEAD_DIM, (h + 1) * HEAD_DIM)
                v_t = jnp.concatenate([r[feat, :] for r in vt_refs], axis=1)
                p_t, linv = _softmax_t(s_t + bias_ref[case, h].astype(F32))
                return jnp.dot(v_t, p_t, preferred_element_type=F32) * linv

            blocks.append((logits, finish))
    return blocks


def _win_blocks(qt_ref, k_refs, vt_refs, bias_ref, sink_ref, t, nt):
    n_sub = TQ // BLOCK
    blocks = []
    for g in range(KVH_B):
        cols = slice(g * LANE_GROUP, (g + 1) * LANE_GROUP)
        for sub in range(n_sub):
            rows = slice(sub * BLOCK, (sub + 1) * BLOCK)
            key_lo = TQ + (sub - 1) * BLOCK
            if sub == 0:
                case = jnp.where(t == 0, 0, 1)
            elif sub == n_sub - 1:
                case = jnp.where(t == nt - 1, 2, 1)
            else:
                case = 1
            for pair in range(G_B // WIN_STACK):

                def logits(g=g, cols=cols, rows=rows, key_lo=key_lo, pair=pair):
                    kwin = jnp.concatenate([r[:, cols] for r in k_refs], axis=0)
                    heads = range(pair * WIN_STACK, (pair + 1) * WIN_STACK)
                    qt_pair = jnp.concatenate([_head_rows(qt_ref, g, j, rows) for j in heads], axis=1)
                    return _qk_t(kwin[key_lo:key_lo + 3 * BLOCK], qt_pair, WIN_QK_ROWS)

                def finish(s_t, g=g, key_lo=key_lo, pair=pair, case=case):
                    vwin_t = jnp.concatenate([r[g * HEAD_DIM:(g + 1) * HEAD_DIM, :] for r in vt_refs], axis=1)
                    p_t, linv = _softmax_t(s_t + bias_ref[case, g, pair].astype(F32),
                                           sink_ref[g * (G_B // WIN_STACK) + pair])
                    v3_t = vwin_t[:, key_lo:key_lo + 3 * BLOCK]
                    return jnp.dot(v3_t, p_t, preferred_element_type=F32) * linv

                blocks.append((logits, finish))
    return blocks


def _run_interleaved(blocks, fillers):
    outs, pending = [], []
    fillers = list(fillers)
    every = max(1, len(blocks) // max(1, len(fillers)))
    for i, (logits, finish) in enumerate(blocks):
        pending.append((logits(), finish))
        if len(pending) > LOOKAHEAD:
            s_t, fin = pending.pop(0)
            outs.append(fin(s_t))
        if fillers and i % every == every - 1:
            fillers.pop(0)()
    outs += [fin(s_t) for s_t, fin in pending]
    for filler in fillers:
        filler()
    return outs


def _assemble_win(outs):
    n_sub, n_pair = TQ // BLOCK, G_B // WIN_STACK
    feats = []
    for g in range(KVH_B):
        per_sub = []
        for sub in range(n_sub):
            pairs = outs[(g * n_sub + sub) * n_pair:(g * n_sub + sub + 1) * n_pair]
            heads = [o[:, i * BLOCK:(i + 1) * BLOCK] for o in pairs for i in range(WIN_STACK)]
            per_sub.append(jnp.concatenate(heads, axis=0))
        feats.append(jnp.concatenate(per_sub, axis=1))
    return jnp.concatenate(feats, axis=0)


def _layer_kernel(xp_ref, xr_ref, w_ref, bd_ref, gains_ref, bias_a_ref, bias_b_ref, sink_ref,
                  woa_ref, wob_ref, wout_ref, y_ref,
                  ka_ring, vat_ring, kb_ring, vbt_ring, qat_buf, za_buf, qbt_buf, zb_buf, ga_buf, gb_buf, *, nt):
    s = pl.program_id(0)
    scratch = (ka_ring, vat_ring, kb_ring, vbt_ring, qat_buf, za_buf, qbt_buf, zb_buf, ga_buf, gb_buf)

    @pl.when(s == 0)
    def _():
        for buf in scratch:
            buf[...] = jnp.zeros(buf.shape, buf.dtype)

    new_kv, new_q = lax.rem(s, KEY_TILES), lax.rem(s, Q_SLOTS)
    early, late = _project_tile(xp_ref, w_ref, bd_ref, gains_ref,
                                ka_ring.at[new_kv], vat_ring.at[new_kv], kb_ring.at[new_kv], vbt_ring.at[new_kv],
                                qat_buf.at[new_q], za_buf.at[new_q], qbt_buf.at[new_q], zb_buf.at[new_q],
                                ga_buf.at[new_q], gb_buf.at[new_q])
    for chunk in early:
        chunk()

    t = lax.rem(jnp.maximum(s - 1, 0), nt)
    case = jnp.where(t == 0, 0, jnp.where(t == nt - 1, 2, 1))
    kv_slots = [lax.rem(s + 1, KEY_TILES), lax.rem(s + 2, KEY_TILES), new_kv]
    old_q = lax.rem(s + 1, Q_SLOTS)
    na = _na_blocks(qat_buf.at[old_q], [ka_ring.at[i] for i in kv_slots], [vat_ring.at[i] for i in kv_slots],
                    bias_a_ref, case)
    win = _win_blocks(qbt_buf.at[old_q], [kb_ring.at[i] for i in kv_slots], [vbt_ring.at[i] for i in kv_slots],
                      bias_b_ref, sink_ref, t, nt)
    outs = _run_interleaved(na + win, late)
    oa = (jnp.concatenate(outs[:len(na)], axis=0).T * za_buf[old_q].astype(F32)).astype(BF16)
    ob = (_assemble_win(outs[len(na):]).T * zb_buf[old_q].astype(F32)).astype(BF16)
    a = jnp.dot(oa, woa_ref[...], preferred_element_type=F32)
    b = jnp.dot(ob, wob_ref[...], preferred_element_type=F32)
    merged = ga_buf[old_q].astype(F32) * a + gb_buf[old_q].astype(F32) * b
    y_ref[...] = xr_ref[...] + jnp.dot(merged.astype(BF16), wout_ref[...], preferred_element_type=F32)


def _layer_call(x2d, w_bf, bd, gains, bias_a, bias_b, sink, woa, wob, wout, seq):
    n_tiles = x2d.shape[0] // TQ
    nt = seq // TQ
    assert seq % TQ == 0 and nt >= 3
    proj_tile = lambda s: (jnp.minimum(s, n_tiles - 1), 0)
    attn_tile = lambda s: (jnp.maximum(s - 1, 0), 0)
    consts = (w_bf, bd, gains, bias_a, bias_b, sink, woa, wob, wout)
    ring = lambda slots, rows, cols: pltpu.VMEM((slots, rows, cols), BF16)
    return pl.pallas_call(
        functools.partial(_layer_kernel, nt=nt),
        grid=(n_tiles + 1,),
        in_specs=[pl.BlockSpec((TQ, D_MODEL), proj_tile), pl.BlockSpec((TQ, D_MODEL), attn_tile)]
                 + [_const_spec(c.shape) for c in consts],
        out_specs=pl.BlockSpec((TQ, D_MODEL), attn_tile),
        out_shape=jax.ShapeDtypeStruct(x2d.shape, F32),
        scratch_shapes=[ring(KEY_TILES, TQ, W_A), ring(KEY_TILES, W_A, TQ), ring(KEY_TILES, TQ, W_B),
                        ring(KEY_TILES, W_KV_B, TQ), ring(Q_SLOTS, W_A, TQ), ring(Q_SLOTS, TQ, W_A),
                        ring(Q_SLOTS, W_B, TQ), ring(Q_SLOTS, TQ, W_B), ring(Q_SLOTS, TQ, D_MODEL),
                        ring(Q_SLOTS, TQ, D_MODEL)],
        compiler_params=pltpu.CompilerParams(dimension_semantics=("arbitrary",),
                                             vmem_limit_bytes=VMEM_LIMIT),
        name="encoder_layer",
    )(x2d, x2d, *consts)


def _toeplitz(v, m, n):
    w = v.shape[-1]
    assert n <= w - 1
    flat = jnp.tile(v, m)[..., :m * (w - 1)]
    return flat.reshape(v.shape[:-1] + (m, w - 1))[..., :n]


def _na_bias(rpb, rows):
    kr = min(NA_ROWS, rows)
    assert kr == NA_ROWS and rows >= 3 * ROWS_PER_TILE
    n_key_rows = KEY_TILES * ROWS_PER_TILE
    rpb = rpb.astype(F32) * LOG2E
    ring_pad = jnp.zeros(rpb.shape[:-1] + (2 * GRID_W - (2 * NA_COLS - 1),), F32)
    ring = jnp.concatenate([rpb[..., NA_COLS - 1:], ring_pad, rpb[..., :NA_COLS - 1]], axis=-1)
    col = _toeplitz(ring, GRID_W, GRID_W)
    first = NA_ROWS - 1 - ROWS_PER_TILE
    per_row = [col[:, first - i:first - i + n_key_rows] for i in range(ROWS_PER_TILE)]
    vals = jnp.transpose(jnp.stack(per_row, axis=1), (0, 1, 3, 2, 4)).reshape(H_A, TQ, KEY_TILES * TQ)
    i = np.arange(ROWS_PER_TILE)[:, None, None, None]
    qc = np.arange(GRID_W)[None, :, None, None]
    j = np.arange(n_key_rows)[None, None, :, None]
    kc = np.arange(GRID_W)[None, None, None, :]
    cs = np.clip(qc - NA_COLS // 2, 0, GRID_W - NA_COLS)
    col_ok = (kc >= cs) & (kc < cs + NA_COLS)
    shape = (ROWS_PER_TILE, GRID_W, n_key_rows, GRID_W)
    row_ok = [
        (j >= ROWS_PER_TILE) & (j < ROWS_PER_TILE + kr),
        (j - i >= 0) & (j - i < kr),
        (j >= 0) & (j < kr),
    ]
    out = []
    for ok in row_ok:
        valid = np.broadcast_to(ok & col_ok, shape).reshape(TQ, KEY_TILES * TQ)
        out.append(jnp.swapaxes(jnp.where(valid[None], vals, NEG), -1, -2))
    return jnp.stack(out).astype(BF16)


def _t5_bucket(rel):
    half = T5_BUCKETS // 2
    max_exact = half // 2
    n = jnp.abs(rel)
    large = max_exact + (jnp.log(jnp.maximum(n, 1).astype(jnp.float32) / max_exact)
                         / math.log(T5_MAX_DIST / max_exact) * (half - max_exact)).astype(jnp.int32)
    large = jnp.minimum(large, half - 1)
    return jnp.where(rel > 0, half, 0) + jnp.where(n < max_exact, n, large)


def _win_bias(t5_table):
    ring = 4 * BLOCK
    u = jnp.arange(ring)
    rel = jnp.where(u < ring // 2, u, u - ring)
    onehot = (_t5_bucket(rel)[:, None] == jnp.arange(T5_BUCKETS)[None, :]).astype(F32)
    by_rel = jnp.dot(onehot, t5_table.astype(F32) * LOG2E, precision=lax.Precision.HIGHEST)
    by_rel = jnp.where((jnp.abs(rel) <= WINDOW)[:, None], by_rel, NEG).T
    bias = _toeplitz(jnp.roll(by_rel, BLOCK, axis=-1), BLOCK, 3 * BLOCK)
    blk = np.arange(3 * BLOCK) // BLOCK
    cases = [jnp.where((blk != 0)[None, None], bias, NEG), bias, jnp.where((blk != 2)[None, None], bias, NEG)]
    shape = (KVH_B, G_B // WIN_STACK, WIN_STACK * BLOCK, 3 * BLOCK)
    return jnp.stack([jnp.swapaxes(c.reshape(shape), -1, -2) for c in cases]).astype(BF16)


def _layer_constants(qn_a, kn_a, qn_b, kn_b):
    q_scale = LOG2E / math.sqrt(HEAD_DIM)
    tile = lambda gvec, s: jnp.tile(gvec.astype(F32) * s, W_A // HEAD_DIM)
    gains = jnp.stack([tile(qn_a, q_scale), tile(kn_a, 1.0), tile(qn_b, q_scale), tile(kn_b, 1.0)])
    lane = np.arange(LANE_GROUP)
    bd = jnp.asarray(lane[:, None] // HEAD_DIM == lane[None, :] // HEAD_DIM, BF16)
    return gains, bd


def _encoder_layer(x, norm_g, w_in, qn_a, kn_a, rpb_a, qn_b, kn_b, sink_b, w_o_a, w_o_b, w_out, t5_table):
    batch, seq, d = x.shape
    assert d == D_MODEL and seq % GRID_W == 0
    x2d = x.reshape(batch * seq, d)
    gains, bd = _layer_constants(qn_a, kn_a, qn_b, kn_b)
    w_bf = (norm_g.astype(F32)[:, None] * w_in).astype(BF16)
    sink = jnp.repeat(sink_b.astype(F32) * LOG2E, BLOCK).reshape(H_B // WIN_STACK, 1, WIN_STACK * BLOCK)
    y = _layer_call(x2d, w_bf, bd, gains, _na_bias(rpb_a, seq // GRID_W), _win_bias(t5_table), sink,
                    w_o_a.astype(BF16), w_o_b.astype(BF16), w_out.astype(BF16), seq)
    return y.reshape(batch, seq, d)


def kernel(x_prompt, x_sample, norm_g, w_in, qn_a, kn_a, rpb_a, qn_b, kn_b, sink_b, w_o_a, w_o_b, w_out, t5_table):
    y_prompt, y_sample = x_prompt, x_sample
    for l in range(norm_g.shape[0]):
        params = (norm_g[l], w_in[l], qn_a[l], kn_a[l], rpb_a[l], qn_b[l], kn_b[l], sink_b[l],
                  w_o_a[l], w_o_b[l], w_out[l], t5_table)
        y_prompt = _encoder_layer(y_prompt, *params)
        y_sample = _encoder_layer(y_sample, *params)
    return (y_prompt, y_sample)
```

```python
import functools
import math

import jax
import jax.numpy as jnp
import numpy as np
from jax import lax
from jax.experimental import pallas as pl
from jax.experimental.pallas import tpu as pltpu

F32 = jnp.float32
BF16 = jnp.bfloat16

D_MODEL = 1024
HEAD_DIM = 64
H_A = 8
H_B = 8
KVH_B = 2
G_B = H_B // KVH_B
W_A = H_A * HEAD_DIM
W_B = H_B * HEAD_DIM
W_KV_B = KVH_B * HEAD_DIM
GRID_W = 64
NA_ROWS = 8
NA_COLS = 16
WINDOW = 128
BLOCK = 128
T5_BUCKETS = 32
T5_MAX_DIST = 128
EPS = 1e-6
NEG = -1e30
LOG2E = math.log2(math.e)

SUBLANES = 8
ONES_ROWS = 16
LANE_GROUP = 256
HEADS_PER_GROUP = LANE_GROUP // HEAD_DIM
OFF_QA, OFF_KA, OFF_VA, OFF_ZA = 0, W_A, 2 * W_A, 3 * W_A
OFF_QB = 4 * W_A
OFF_KB = OFF_QB + W_B
OFF_VB = OFF_KB + W_KV_B
OFF_ZB = OFF_VB + W_KV_B
OFF_GA = OFF_ZB + W_B
OFF_GB = OFF_GA + D_MODEL
D_IN = OFF_GB + D_MODEL

TQ = 256
ROWS_PER_TILE = TQ // GRID_W
KEY_TILES = 3
LAG = 1
KV_SLOTS = KEY_TILES + LAG - 1
Q_SLOTS = LAG + 1
WIN_STACK = 2
NA_QK_ROWS = 256
WIN_QK_ROWS = 128
LOOKAHEAD = 1
VMEM_LIMIT = 56 * 1024 * 1024


def _const_spec(shape):
    return pl.BlockSpec(shape, lambda *_: (0,) * len(shape))


def _project_tile(x_ref, w_ref, bd_ref, gains_ref,
                  ka_ref, vat_ref, kb_ref, vbt_ref, qat_ref, za_ref, qbt_ref, zb_ref, ga_ref, gb_ref):
    x = x_ref[...]
    inv_rms = lax.rsqrt(jnp.mean(x * x, axis=-1, keepdims=True) + EPS)
    xb = x.astype(BF16)

    def proj(lo, width):
        return jnp.dot(xb, w_ref[:, lo:lo + width], preferred_element_type=F32) * inv_rms

    def head_norm(p, gain):
        width = p.shape[-1]
        sq = (p * p).astype(BF16)
        parts = []
        for lo in range(0, width, LANE_GROUP):
            w = min(LANE_GROUP, width - lo)
            parts.append(jnp.dot(sq[:, lo:lo + w], bd_ref[:w, :w], preferred_element_type=F32))
        ssq = parts[0] if len(parts) == 1 else jnp.concatenate(parts, axis=-1)
        return p * lax.rsqrt(ssq * (1.0 / HEAD_DIM) + EPS) * gain

    def silu(z):
        return z * jax.nn.sigmoid(z)

    def spread_kv(kv):
        swapped = pltpu.roll(kv, HEAD_DIM, 1)
        low = lax.broadcasted_iota(jnp.int32, kv.shape, 1) < HEAD_DIM
        first = jnp.where(low, kv, swapped).astype(BF16)
        second = jnp.where(low, swapped, kv).astype(BF16)
        return jnp.concatenate([first, first, second, second], axis=-1)

    def store(ref, value, c=None):
        if c is None:
            ref[...] = value().astype(BF16)
        else:
            ref[:, c:c + W_A] = value().astype(BF16)

    early = [
        lambda: store(ka_ref, lambda: head_norm(proj(OFF_KA, W_A), gains_ref[1:2, :])),
        lambda: store(vat_ref, lambda: proj(OFF_VA, W_A).T),
        lambda: store(kb_ref, lambda: spread_kv(head_norm(proj(OFF_KB, W_KV_B), gains_ref[3:4, :W_KV_B]))),
        lambda: store(vbt_ref, lambda: proj(OFF_VB, W_KV_B).T),
        lambda: store(qat_ref, lambda: head_norm(proj(OFF_QA, W_A), gains_ref[0:1, :]).T),
        lambda: store(qbt_ref, lambda: head_norm(proj(OFF_QB, W_B), gains_ref[2:3, :]).T),
    ]
    late = [
        lambda: store(za_ref, lambda: silu(proj(OFF_ZA, W_A))),
        lambda: store(zb_ref, lambda: silu(proj(OFF_ZB, W_B))),
    ]
    for c in range(0, D_MODEL, W_A):
        late.append(lambda c=c: store(ga_ref, lambda: jax.nn.sigmoid(proj(OFF_GA + c, W_A)), c))
        late.append(lambda c=c: store(gb_ref, lambda: jax.nn.sigmoid(proj(OFF_GB + c, W_A)), c))
    return early, late


def _reduce_keys(x, combine, reduce):
    while x.shape[0] % (2 * SUBLANES) == 0:
        half = x.shape[0] // 2
        x = combine(x[:half], x[half:])
    return reduce(x, axis=0, keepdims=True)


def _attend_t(s_t, v_t, sink=None):
    m = _reduce_keys(s_t, jnp.maximum, jnp.max)
    if sink is not None:
        m = jnp.maximum(m, sink)
    p = jnp.exp2((s_t - m).astype(BF16))
    v_ones = jnp.concatenate([v_t, jnp.ones((ONES_ROWS, v_t.shape[1]), BF16)], axis=0)
    o = jnp.dot(v_ones, p, preferred_element_type=F32)
    l = o[HEAD_DIM:HEAD_DIM + 1]
    if sink is not None:
        l = l + jnp.exp2(sink - m)
    return o[:HEAD_DIM] * (1.0 / l)


def _head_rows(qt_ref, g, j, queries):
    lo = g * LANE_GROUP + j * HEAD_DIM
    head = qt_ref[lo:lo + HEAD_DIM, queries]
    above = jnp.zeros((j * HEAD_DIM, head.shape[1]), BF16)
    below = jnp.zeros(((HEADS_PER_GROUP - 1 - j) * HEAD_DIM, head.shape[1]), BF16)
    return jnp.concatenate([part for part in (above, head, below) if part.shape[0]], axis=0)


def _qk_t(keys, qt_masked, chunk):
    parts = [jnp.dot(keys[r:r + chunk], qt_masked, preferred_element_type=F32)
             for r in range(0, keys.shape[0], chunk)]
    return jnp.concatenate(parts, axis=0)


def _na_blocks(qt_ref, k_refs, vt_refs, bias_ref, case):
    blocks = []
    for g in range(W_A // LANE_GROUP):
        cols = slice(g * LANE_GROUP, (g + 1) * LANE_GROUP)
        for j in range(HEADS_PER_GROUP):
            h = g * HEADS_PER_GROUP + j

            def logits(g=g, cols=cols, j=j):
                k4 = jnp.concatenate([r[:, cols] for r in k_refs], axis=0)
                return _qk_t(k4, _head_rows(qt_ref, g, j, slice(None)), NA_QK_ROWS)

            def finish(s_t, h=h):
                feat = slice(h * HEAD_DIM, (h + 1) * HEAD_DIM)
                v_t = jnp.concatenate([r[feat, :] for r in vt_refs], axis=1)
                return _attend_t(s_t + bias_ref[case, h].astype(F32), v_t)

            blocks.append((logits, finish))
    return blocks


def _win_blocks(qt_ref, k_refs, vt_refs, bias_ref, sink_ref, t, nt):
    n_sub = TQ // BLOCK
    blocks = []
    for g in range(KVH_B):
        cols = slice(g * LANE_GROUP, (g + 1) * LANE_GROUP)
        for sub in range(n_sub):
            rows = slice(sub * BLOCK, (sub + 1) * BLOCK)
            key_lo = TQ + (sub - 1) * BLOCK
            if sub == 0:
                case = jnp.where(t == 0, 0, 1)
            elif sub == n_sub - 1:
                case = jnp.where(t == nt - 1, 2, 1)
            else:
                case = 1
            for pair in range(G_B // WIN_STACK):

                def logits(g=g, cols=cols, rows=rows, key_lo=key_lo, pair=pair):
                    kwin = jnp.concatenate([r[:, cols] for r in k_refs], axis=0)
                    heads = range(pair * WIN_STACK, (pair + 1) * WIN_STACK)
                    qt_pair = jnp.concatenate([_head_rows(qt_ref, g, j, rows) for j in heads], axis=1)
                    return _qk_t(kwin[key_lo:key_lo + 3 * BLOCK], qt_pair, WIN_QK_ROWS)

                def finish(s_t, g=g, key_lo=key_lo, pair=pair, case=case):
                    vwin_t = jnp.concatenate([r[g * HEAD_DIM:(g + 1) * HEAD_DIM, :] for r in vt_refs], axis=1)
                    v3_t = vwin_t[:, key_lo:key_lo + 3 * BLOCK]
                    return _attend_t(s_t + bias_ref[case, g, pair].astype(F32), v3_t,
                                     sink_ref[g * (G_B // WIN_STACK) + pair])

                blocks.append((logits, finish))
    return blocks


def _run_interleaved(blocks, fillers):
    outs, pending = [], []
    fillers = list(fillers)
    every = max(1, len(blocks) // max(1, len(fillers)))
    for i, (logits, finish) in enumerate(blocks):
        pending.append((logits(), finish))
        if len(pending) > LOOKAHEAD:
            s_t, fin = pending.pop(0)
            outs.append(fin(s_t))
        if fillers and i % every == every - 1:
            fillers.pop(0)()
    outs += [fin(s_t) for s_t, fin in pending]
    for filler in fillers:
        filler()
    return outs


def _assemble_win(outs):
    n_sub, n_pair = TQ // BLOCK, G_B // WIN_STACK
    feats = []
    for g in range(KVH_B):
        per_sub = []
        for sub in range(n_sub):
            pairs = outs[(g * n_sub + sub) * n_pair:(g * n_sub + sub + 1) * n_pair]
            heads = [o[:, i * BLOCK:(i + 1) * BLOCK] for o in pairs for i in range(WIN_STACK)]
            per_sub.append(jnp.concatenate(heads, axis=0))
        feats.append(jnp.concatenate(per_sub, axis=1))
    return jnp.concatenate(feats, axis=0)


def _layer_kernel(xp_ref, xr_ref, w_ref, bd_ref, gains_ref, bias_a_ref, bias_b_ref, sink_ref,
                  woa_ref, wob_ref, wout_ref, y_ref,
                  ka_ring, vat_ring, kb_ring, vbt_ring, qat_buf, za_buf, qbt_buf, zb_buf, ga_buf, gb_buf, *, nt):
    s = pl.program_id(0)
    scratch = (ka_ring, vat_ring, kb_ring, vbt_ring, qat_buf, za_buf, qbt_buf, zb_buf, ga_buf, gb_buf)

    @pl.when(s == 0)
    def _():
        for buf in scratch:
            buf[...] = jnp.zeros(buf.shape, buf.dtype)

    new_kv, new_q = lax.rem(s, KV_SLOTS), lax.rem(s, Q_SLOTS)
    early, late = _project_tile(xp_ref, w_ref, bd_ref, gains_ref,
                                ka_ring.at[new_kv], vat_ring.at[new_kv], kb_ring.at[new_kv], vbt_ring.at[new_kv],
                                qat_buf.at[new_q], za_buf.at[new_q], qbt_buf.at[new_q], zb_buf.at[new_q],
                                ga_buf.at[new_q], gb_buf.at[new_q])
    if LAG == 1:
        for chunk in early:
            chunk()
        fillers = late
    else:
        fillers = early + late

    t = lax.rem(jnp.maximum(s - LAG, 0), nt)
    case = jnp.where(t == 0, 0, jnp.where(t == nt - 1, 2, 1))
    kv_slots = [lax.rem(s + KV_SLOTS - LAG + d, KV_SLOTS) for d in (-1, 0, 1)]
    old_q = lax.rem(s + Q_SLOTS - LAG, Q_SLOTS)
    na = _na_blocks(qat_buf.at[old_q], [ka_ring.at[i] for i in kv_slots], [vat_ring.at[i] for i in kv_slots],
                    bias_a_ref, case)
    win = _win_blocks(qbt_buf.at[old_q], [kb_ring.at[i] for i in kv_slots], [vbt_ring.at[i] for i in kv_slots],
                      bias_b_ref, sink_ref, t, nt)
    outs = _run_interleaved(na + win, fillers)
    oa = (jnp.concatenate(outs[:len(na)], axis=0).T * za_buf[old_q].astype(F32)).astype(BF16)
    ob = (_assemble_win(outs[len(na):]).T * zb_buf[old_q].astype(F32)).astype(BF16)
    a = jnp.dot(oa, woa_ref[...], preferred_element_type=F32)
    b = jnp.dot(ob, wob_ref[...], preferred_element_type=F32)
    merged = ga_buf[old_q].astype(F32) * a + gb_buf[old_q].astype(F32) * b
    y_ref[...] = xr_ref[...] + jnp.dot(merged.astype(BF16), wout_ref[...], preferred_element_type=F32)


def _layer_call(x2d, w_bf, bd, gains, bias_a, bias_b, sink, woa, wob, wout, seq):
    n_tiles = x2d.shape[0] // TQ
    nt = seq // TQ
    assert seq % TQ == 0 and nt >= 3
    proj_tile = lambda s: (jnp.minimum(s, n_tiles - 1), 0)
    attn_tile = lambda s: (jnp.maximum(s - LAG, 0), 0)
    consts = (w_bf, bd, gains, bias_a, bias_b, sink, woa, wob, wout)
    ring = lambda slots, rows, cols: pltpu.VMEM((slots, rows, cols), BF16)
    return pl.pallas_call(
        functools.partial(_layer_kernel, nt=nt),
        grid=(n_tiles + LAG,),
        in_specs=[pl.BlockSpec((TQ, D_MODEL), proj_tile), pl.BlockSpec((TQ, D_MODEL), attn_tile)]
                 + [_const_spec(c.shape) for c in consts],
        out_specs=pl.BlockSpec((TQ, D_MODEL), attn_tile),
        out_shape=jax.ShapeDtypeStruct(x2d.shape, F32),
        scratch_shapes=[ring(KV_SLOTS, TQ, W_A), ring(KV_SLOTS, W_A, TQ), ring(KV_SLOTS, TQ, W_B),
                        ring(KV_SLOTS, W_KV_B, TQ), ring(Q_SLOTS, W_A, TQ), ring(Q_SLOTS, TQ, W_A),
                        ring(Q_SLOTS, W_B, TQ), ring(Q_SLOTS, TQ, W_B), ring(Q_SLOTS, TQ, D_MODEL),
                        ring(Q_SLOTS, TQ, D_MODEL)],
        compiler_params=pltpu.CompilerParams(dimension_semantics=("arbitrary",),
                                             vmem_limit_bytes=VMEM_LIMIT),
        name="encoder_layer",
    )(x2d, x2d, *consts)


def _toeplitz(v, m, n):
    w = v.shape[-1]
    assert n <= w - 1
    flat = jnp.tile(v, m)[..., :m * (w - 1)]
    return flat.reshape(v.shape[:-1] + (m, w - 1))[..., :n]


def _na_bias(rpb, rows):
    kr = min(NA_ROWS, rows)
    assert kr == NA_ROWS and rows >= 3 * ROWS_PER_TILE
    n_key_rows = KEY_TILES * ROWS_PER_TILE
    rpb = rpb.astype(F32) * LOG2E
    ring_pad = jnp.zeros(rpb.shape[:-1] + (2 * GRID_W - (2 * NA_COLS - 1),), F32)
    ring = jnp.concatenate([rpb[..., NA_COLS - 1:], ring_pad, rpb[..., :NA_COLS - 1]], axis=-1)
    col = _toeplitz(ring, GRID_W, GRID_W)
    first = NA_ROWS - 1 - ROWS_PER_TILE
    per_row = [col[:, first - i:first - i + n_key_rows] for i in range(ROWS_PER_TILE)]
    vals = jnp.transpose(jnp.stack(per_row, axis=1), (0, 1, 3, 2, 4)).reshape(H_A, TQ, KEY_TILES * TQ)
    i = np.arange(ROWS_PER_TILE)[:, None, None, None]
    qc = np.arange(GRID_W)[None, :, None, None]
    j = np.arange(n_key_rows)[None, None, :, None]
    kc = np.arange(GRID_W)[None, None, None, :]
    cs = np.clip(qc - NA_COLS // 2, 0, GRID_W - NA_COLS)
    col_ok = (kc >= cs) & (kc < cs + NA_COLS)
    shape = (ROWS_PER_TILE, GRID_W, n_key_rows, GRID_W)
    row_ok = [
        (j >= ROWS_PER_TILE) & (j < ROWS_PER_TILE + kr),
        (j - i >= 0) & (j - i < kr),
        (j >= 0) & (j < kr),
    ]
    out = []
    for ok in row_ok:
        valid = np.broadcast_to(ok & col_ok, shape).reshape(TQ, KEY_TILES * TQ)
        out.append(jnp.swapaxes(jnp.where(valid[None], vals, NEG), -1, -2))
    return jnp.stack(out).astype(BF16)


def _t5_bucket(rel):
    half = T5_BUCKETS // 2
    max_exact = half // 2
    n = jnp.abs(rel)
    large = max_exact + (jnp.log(jnp.maximum(n, 1).astype(jnp.float32) / max_exact)
                         / math.log(T5_MAX_DIST / max_exact) * (half - max_exact)).astype(jnp.int32)
    large = jnp.minimum(large, half - 1)
    return jnp.where(rel > 0, half, 0) + jnp.where(n < max_exact, n, large)


def _win_bias(t5_table):
    ring = 4 * BLOCK
    u = jnp.arange(ring)
    rel = jnp.where(u < ring // 2, u, u - ring)
    onehot = (_t5_bucket(rel)[:, None] == jnp.arange(T5_BUCKETS)[None, :]).astype(F32)
    by_rel = jnp.dot(onehot, t5_table.astype(F32) * LOG2E, precision=lax.Precision.HIGHEST)
    by_rel = jnp.where((jnp.abs(rel) <= WINDOW)[:, None], by_rel, NEG).T
    bias = _toeplitz(jnp.roll(by_rel, BLOCK, axis=-1), BLOCK, 3 * BLOCK)
    blk = np.arange(3 * BLOCK) // BLOCK
    cases = [jnp.where((blk != 0)[None, None], bias, NEG), bias, jnp.where((blk != 2)[None, None], bias, NEG)]
    shape = (KVH_B, G_B // WIN_STACK, WIN_STACK * BLOCK, 3 * BLOCK)
    return jnp.stack([jnp.swapaxes(c.reshape(shape), -1, -2) for c in cases]).astype(BF16)


def _layer_constants(qn_a, kn_a, qn_b, kn_b):
    q_scale = LOG2E / math.sqrt(HEAD_DIM)
    tile = lambda gvec, s: jnp.tile(gvec.astype(F32) * s, W_A // HEAD_DIM)
    gains = jnp.stack([tile(qn_a, q_scale), tile(kn_a, 1.0), tile(qn_b, q_scale), tile(kn_b, 1.0)])
    lane = np.arange(LANE_GROUP)
    bd = jnp.asarray(lane[:, None] // HEAD_DIM == lane[None, :] // HEAD_DIM, BF16)
    return gains, bd


def _encoder_layer(x, norm_g, w_in, qn_a, kn_a, rpb_a, qn_b, kn_b, sink_b, w_o_a, w_o_b, w_out, t5_table):
    batch, seq, d = x.shape
    assert d == D_MODEL and seq % GRID_W == 0
    x2d = x.reshape(batch * seq, d)
    gains, bd = _layer_constants(qn_a, kn_a, qn_b, kn_b)
    w_bf = (norm_g.astype(F32)[:, None] * w_in).astype(BF16)
    sink = jnp.repeat(sink_b.astype(F32) * LOG2E, BLOCK).reshape(H_B // WIN_STACK, 1, WIN_STACK * BLOCK)
    y = _layer_call(x2d, w_bf, bd, gains, _na_bias(rpb_a, seq // GRID_W), _win_bias(t5_table), sink,
                    w_o_a.astype(BF16), w_o_b.astype(BF16), w_out.astype(BF16), seq)
    return y.reshape(batch, seq, d)


def kernel(x_prompt, x_sample, norm_g, w_in, qn_a, kn_a, rpb_a, qn_b, kn_b, sink_b, w_o_a, w_o_b, w_out, t5_table):
    y_prompt, y_sample = x_prompt, x_sample
    for l in range(norm_g.shape[0]):
        params = (norm_g[l], w_in[l], qn_a[l], kn_a[l], rpb_a[l], qn_b[l], kn_b[l], sink_b[l],
                  w_o_a[l], w_o_b[l], w_out[l], t5_table)
        y_prompt = _encoder_layer(y_prompt, *params)
        y_sample = _encoder_layer(y_sample, *params)
    return (y_prompt, y_sample)
```

```python
import functools
import math

import jax
import jax.numpy as jnp
import numpy as np
from jax import lax
from jax.experimental import pallas as pl
from jax.experimental.pallas import tpu as pltpu

F32 = jnp.float32
BF16 = jnp.bfloat16

D_MODEL = 1024
HEAD_DIM = 64
H_A = 8
H_B = 8
KVH_B = 2
G_B = H_B // KVH_B
W_A = H_A * HEAD_DIM
W_B = H_B * HEAD_DIM
W_KV_B = KVH_B * HEAD_DIM
GRID_W = 64
NA_ROWS = 8
NA_COLS = 16
WINDOW = 128
BLOCK = 128
T5_BUCKETS = 32
T5_MAX_DIST = 128
EPS = 1e-6
NEG = -1e30
LOG2E = math.log2(math.e)

SUBLANES = 8
ONES_ROWS = 16
LANE_GROUP = 256
HEADS_PER_GROUP = LANE_GROUP // HEAD_DIM
OFF_QA, OFF_KA, OFF_VA, OFF_ZA = 0, W_A, 2 * W_A, 3 * W_A
OFF_QB = 4 * W_A
OFF_KB = OFF_QB + W_B
OFF_VB = OFF_KB + W_KV_B
OFF_ZB = OFF_VB + W_KV_B
OFF_GA = OFF_ZB + W_B
OFF_GB = OFF_GA + D_MODEL
D_IN = OFF_GB + D_MODEL

TQ = 256
ROWS_PER_TILE = TQ // GRID_W
KEY_TILES = 3
LAG = 1
KV_SLOTS = KEY_TILES + LAG - 1
Q_SLOTS = LAG + 1
WIN_STACK = 2
NA_QK_ROWS = 256
WIN_QK_ROWS = 128
LOOKAHEAD = 2
VMEM_LIMIT = 56 * 1024 * 1024


def _const_spec(shape):
    return pl.BlockSpec(shape, lambda *_: (0,) * len(shape))


def _project_tile(x_ref, w_ref, bd_ref, gains_ref,
                  ka_ref, vat_ref, kb_ref, vbt_ref, qat_ref, za_ref, qbt_ref, zb_ref, ga_ref, gb_ref):
    x = x_ref[...]
    inv_rms = lax.rsqrt(jnp.mean(x * x, axis=-1, keepdims=True) + EPS)
    xb = x.astype(BF16)

    def proj(lo, width):
        return jnp.dot(xb, w_ref[:, lo:lo + width], preferred_element_type=F32) * inv_rms

    def head_norm(p, gain):
        width = p.shape[-1]
        sq = (p * p).astype(BF16)
        parts = []
        for lo in range(0, width, LANE_GROUP):
            w = min(LANE_GROUP, width - lo)
            parts.append(jnp.dot(sq[:, lo:lo + w], bd_ref[:w, :w], preferred_element_type=F32))
        ssq = parts[0] if len(parts) == 1 else jnp.concatenate(parts, axis=-1)
        return p * lax.rsqrt(ssq * (1.0 / HEAD_DIM) + EPS) * gain

    def silu(z):
        return z * jax.nn.sigmoid(z)

    def spread_kv(kv):
        swapped = pltpu.roll(kv, HEAD_DIM, 1)
        low = lax.broadcasted_iota(jnp.int32, kv.shape, 1) < HEAD_DIM
        first = jnp.where(low, kv, swapped).astype(BF16)
        second = jnp.where(low, swapped, kv).astype(BF16)
        return jnp.concatenate([first, first, second, second], axis=-1)

    def store(ref, value, c=None):
        if c is None:
            ref[...] = value().astype(BF16)
        else:
            ref[:, c:c + W_A] = value().astype(BF16)

    early = [
        lambda: store(ka_ref, lambda: head_norm(proj(OFF_KA, W_A), gains_ref[1:2, :])),
        lambda: store(vat_ref, lambda: proj(OFF_VA, W_A).T),
        lambda: store(kb_ref, lambda: spread_kv(head_norm(proj(OFF_KB, W_KV_B), gains_ref[3:4, :W_KV_B]))),
        lambda: store(vbt_ref, lambda: proj(OFF_VB, W_KV_B).T),
    ]
    late = [
        lambda: store(qat_ref, lambda: head_norm(proj(OFF_QA, W_A), gains_ref[0:1, :]).T),
        lambda: store(qbt_ref, lambda: head_norm(proj(OFF_QB, W_B), gains_ref[2:3, :]).T),
        lambda: store(za_ref, lambda: silu(proj(OFF_ZA, W_A))),
        lambda: store(zb_ref, lambda: silu(proj(OFF_ZB, W_B))),
    ]
    for c in range(0, D_MODEL, W_A):
        late.append(lambda c=c: store(ga_ref, lambda: jax.nn.sigmoid(proj(OFF_GA + c, W_A)), c))
        late.append(lambda c=c: store(gb_ref, lambda: jax.nn.sigmoid(proj(OFF_GB + c, W_A)), c))
    return early, late


def _reduce_keys(x, combine, reduce):
    while x.shape[0] % (2 * SUBLANES) == 0:
        half = x.shape[0] // 2
        x = combine(x[:half], x[half:])
    return reduce(x, axis=0, keepdims=True)


def _attend_t(s_t, v_t, sink=None):
    m = _reduce_keys(s_t, jnp.maximum, jnp.max)
    if sink is not None:
        m = jnp.maximum(m, sink)
    p = jnp.exp2((s_t - m).astype(BF16))
    v_ones = jnp.concatenate([v_t, jnp.ones((ONES_ROWS, v_t.shape[1]), BF16)], axis=0)
    o = jnp.dot(v_ones, p, preferred_element_type=F32)
    l = o[HEAD_DIM:HEAD_DIM + 1]
    if sink is not None:
        l = l + jnp.exp2(sink - m)
    return o[:HEAD_DIM] * (1.0 / l)


def _head_rows(qt_ref, g, j, queries):
    lo = g * LANE_GROUP + j * HEAD_DIM
    head = qt_ref[lo:lo + HEAD_DIM, queries]
    above = jnp.zeros((j * HEAD_DIM, head.shape[1]), BF16)
    below = jnp.zeros(((HEADS_PER_GROUP - 1 - j) * HEAD_DIM, head.shape[1]), BF16)
    return jnp.concatenate([part for part in (above, head, below) if part.shape[0]], axis=0)


def _qk_t(keys, qt_masked, chunk):
    parts = [jnp.dot(keys[r:r + chunk], qt_masked, preferred_element_type=F32)
             for r in range(0, keys.shape[0], chunk)]
    return jnp.concatenate(parts, axis=0)


def _na_blocks(qt_ref, k_refs, vt_refs, bias_ref, case):
    blocks = []
    for g in range(W_A // LANE_GROUP):
        cols = slice(g * LANE_GROUP, (g + 1) * LANE_GROUP)
        for j in range(HEADS_PER_GROUP):
            h = g * HEADS_PER_GROUP + j

            def logits(g=g, cols=cols, j=j):
                k4 = jnp.concatenate([r[:, cols] for r in k_refs], axis=0)
                return _qk_t(k4, _head_rows(qt_ref, g, j, slice(None)), NA_QK_ROWS)

            def finish(s_t, h=h):
                feat = slice(h * HEAD_DIM, (h + 1) * HEAD_DIM)
                v_t = jnp.concatenate([r[feat, :] for r in vt_refs], axis=1)
                return _attend_t(s_t + bias_ref[case, h].astype(F32), v_t)

            blocks.append((logits, finish))
    return blocks


def _win_blocks(qt_ref, k_refs, vt_refs, bias_ref, sink_ref, t, nt):
    n_sub = TQ // BLOCK
    blocks = []
    for g in range(KVH_B):
        cols = slice(g * LANE_GROUP, (g + 1) * LANE_GROUP)
        for sub in range(n_sub):
            rows = slice(sub * BLOCK, (sub + 1) * BLOCK)
            key_lo = TQ + (sub - 1) * BLOCK
            if sub == 0:
                case = jnp.where(t == 0, 0, 1)
            elif sub == n_sub - 1:
                case = jnp.where(t == nt - 1, 2, 1)
            else:
                case = 1
            for pair in range(G_B // WIN_STACK):

                def logits(g=g, cols=cols, rows=rows, key_lo=key_lo, pair=pair):
                    kwin = jnp.concatenate([r[:, cols] for r in k_refs], axis=0)
                    heads = range(pair * WIN_STACK, (pair + 1) * WIN_STACK)
                    qt_pair = jnp.concatenate([_head_rows(qt_ref, g, j, rows) for j in heads], axis=1)
                    return _qk_t(kwin[key_lo:key_lo + 3 * BLOCK], qt_pair, WIN_QK_ROWS)

                def finish(s_t, g=g, key_lo=key_lo, pair=pair, case=case):
                    vwin_t = jnp.concatenate([r[g * HEAD_DIM:(g + 1) * HEAD_DIM, :] for r in vt_refs], axis=1)
                    v3_t = vwin_t[:, key_lo:key_lo + 3 * BLOCK]
                    return _attend_t(s_t + bias_ref[case, g, pair].astype(F32), v3_t,
                                     sink_ref[g * (G_B // WIN_STACK) + pair])

                blocks.append((logits, finish))
    return blocks


def _run_interleaved(blocks, fillers):
    outs, pending = [], []
    fillers = list(fillers)
    every = max(1, len(blocks) // max(1, len(fillers)))
    for i, (logits, finish) in enumerate(blocks):
        pending.append((logits(), finish))
        if len(pending) > LOOKAHEAD:
            s_t, fin = pending.pop(0)
            outs.append(fin(s_t))
        if fillers and i % every == every - 1:
            fillers.pop(0)()
    outs += [fin(s_t) for s_t, fin in pending]
    for filler in fillers:
        filler()
    return outs


def _assemble_win(outs):
    n_sub, n_pair = TQ // BLOCK, G_B // WIN_STACK
    feats = []
    for g in range(KVH_B):
        per_sub = []
        for sub in range(n_sub):
            pairs = outs[(g * n_sub + sub) * n_pair:(g * n_sub + sub + 1) * n_pair]
            heads = [o[:, i * BLOCK:(i + 1) * BLOCK] for o in pairs for i in range(WIN_STACK)]
            per_sub.append(jnp.concatenate(heads, axis=0))
        feats.append(jnp.concatenate(per_sub, axis=1))
    return jnp.concatenate(feats, axis=0)


def _layer_kernel(xp_ref, xr_ref, w_ref, bd_ref, gains_ref, bias_a_ref, bias_b_ref, sink_ref,
                  woa_ref, wob_ref, wout_ref, y_ref,
                  ka_ring, vat_ring, kb_ring, vbt_ring, qat_buf, za_buf, qbt_buf, zb_buf, ga_buf, gb_buf, *, nt):
    s = pl.program_id(0)
    scratch = (ka_ring, vat_ring, kb_ring, vbt_ring, qat_buf, za_buf, qbt_buf, zb_buf, ga_buf, gb_buf)

    @pl.when(s == 0)
    def _():
        for buf in scratch:
            buf[...] = jnp.zeros(buf.shape, buf.dtype)

    new_kv, new_q = lax.rem(s, KV_SLOTS), lax.rem(s, Q_SLOTS)
    early, late = _project_tile(xp_ref, w_ref, bd_ref, gains_ref,
                                ka_ring.at[new_kv], vat_ring.at[new_kv], kb_ring.at[new_kv], vbt_ring.at[new_kv],
                                qat_buf.at[new_q], za_buf.at[new_q], qbt_buf.at[new_q], zb_buf.at[new_q],
                                ga_buf.at[new_q], gb_buf.at[new_q])
    if LAG == 1:
        for chunk in early:
            chunk()
        fillers = late
    else:
        fillers = early + late

    t = lax.rem(jnp.maximum(s - LAG, 0), nt)
    case = jnp.where(t == 0, 0, jnp.where(t == nt - 1, 2, 1))
    kv_slots = [lax.rem(s + KV_SLOTS - LAG + d, KV_SLOTS) for d in (-1, 0, 1)]
    old_q = lax.rem(s + Q_SLOTS - LAG, Q_SLOTS)
    na = _na_blocks(qat_buf.at[old_q], [ka_ring.at[i] for i in kv_slots], [vat_ring.at[i] for i in kv_slots],
                    bias_a_ref, case)
    win = _win_blocks(qbt_buf.at[old_q], [kb_ring.at[i] for i in kv_slots], [vbt_ring.at[i] for i in kv_slots],
                      bias_b_ref, sink_ref, t, nt)
    outs = _run_interleaved(na + win, fillers)
    oa = (jnp.concatenate(outs[:len(na)], axis=0).T * za_buf[old_q].astype(F32)).astype(BF16)
    ob = (_assemble_win(outs[len(na):]).T * zb_buf[old_q].astype(F32)).astype(BF16)
    a = jnp.dot(oa, woa_ref[...], preferred_element_type=F32)
    b = jnp.dot(ob, wob_ref[...], preferred_element_type=F32)
    merged = ga_buf[old_q].astype(F32) * a + gb_buf[old_q].astype(F32) * b
    y_ref[...] = xr_ref[...] + jnp.dot(merged.astype(BF16), wout_ref[...], preferred_element_type=F32)


def _layer_call(x2d, w_bf, bd, gains, bias_a, bias_b, sink, woa, wob, wout, seq):
    n_tiles = x2d.shape[0] // TQ
    nt = seq // TQ
    assert seq % TQ == 0 and nt >= 3
    proj_tile = lambda s: (jnp.minimum(s, n_tiles - 1), 0)
    attn_tile = lambda s: (jnp.maximum(s - LAG, 0), 0)
    consts = (w_bf, bd, gains, bias_a, bias_b, sink, woa, wob, wout)
    ring = lambda slots, rows, cols: pltpu.VMEM((slots, rows, cols), BF16)
    return pl.pallas_call(
        functools.partial(_layer_kernel, nt=nt),
        grid=(n_tiles + LAG,),
        in_specs=[pl.BlockSpec((TQ, D_MODEL), proj_tile), pl.BlockSpec((TQ, D_MODEL), attn_tile)]
                 + [_const_spec(c.shape) for c in consts],
        out_specs=pl.BlockSpec((TQ, D_MODEL), attn_tile),
        out_shape=jax.ShapeDtypeStruct(x2d.shape, F32),
        scratch_shapes=[ring(KV_SLOTS, TQ, W_A), ring(KV_SLOTS, W_A, TQ), ring(KV_SLOTS, TQ, W_B),
                        ring(KV_SLOTS, W_KV_B, TQ), ring(Q_SLOTS, W_A, TQ), ring(Q_SLOTS, TQ, W_A),
                        ring(Q_SLOTS, W_B, TQ), ring(Q_SLOTS, TQ, W_B), ring(Q_SLOTS, TQ, D_MODEL),
                        ring(Q_SLOTS, TQ, D_MODEL)],
        compiler_params=pltpu.CompilerParams(dimension_semantics=("arbitrary",),
                                             vmem_limit_bytes=VMEM_LIMIT),
        name="encoder_layer",
    )(x2d, x2d, *consts)


def _toeplitz(v, m, n):
    w = v.shape[-1]
    assert n <= w - 1
    flat = jnp.tile(v, m)[..., :m * (w - 1)]
    return flat.reshape(v.shape[:-1] + (m, w - 1))[..., :n]


def _na_bias(rpb, rows):
    kr = min(NA_ROWS, rows)
    assert kr == NA_ROWS and rows >= 3 * ROWS_PER_TILE
    n_key_rows = KEY_TILES * ROWS_PER_TILE
    rpb = rpb.astype(F32) * LOG2E
    ring_pad = jnp.zeros(rpb.shape[:-1] + (2 * GRID_W - (2 * NA_COLS - 1),), F32)
    ring = jnp.concatenate([rpb[..., NA_COLS - 1:], ring_pad, rpb[..., :NA_COLS - 1]], axis=-1)
    col = _toeplitz(ring, GRID_W, GRID_W)
    first = NA_ROWS - 1 - ROWS_PER_TILE
    per_row = [col[:, first - i:first - i + n_key_rows] for i in range(ROWS_PER_TILE)]
    vals = jnp.transpose(jnp.stack(per_row, axis=1), (0, 1, 3, 2, 4)).reshape(H_A, TQ, KEY_TILES * TQ)
    i = np.arange(ROWS_PER_TILE)[:, None, None, None]
    qc = np.arange(GRID_W)[None, :, None, None]
    j = np.arange(n_key_rows)[None, None, :, None]
    kc = np.arange(GRID_W)[None, None, None, :]
    cs = np.clip(qc - NA_COLS // 2, 0, GRID_W - NA_COLS)
    col_ok = (kc >= cs) & (kc < cs + NA_COLS)
    shape = (ROWS_PER_TILE, GRID_W, n_key_rows, GRID_W)
    row_ok = [
        (j >= ROWS_PER_TILE) & (j < ROWS_PER_TILE + kr),
        (j - i >= 0) & (j - i < kr),
        (j >= 0) & (j < kr),
    ]
    out = []
    for ok in row_ok:
        valid = np.broadcast_to(ok & col_ok, shape).reshape(TQ, KEY_TILES * TQ)
        out.append(jnp.swapaxes(jnp.where(valid[None], vals, NEG), -1, -2))
    return jnp.stack(out).astype(BF16)


def _t5_bucket(rel):
    half = T5_BUCKETS // 2
    max_exact = half // 2
    n = jnp.abs(rel)
    large = max_exact + (jnp.log(jnp.maximum(n, 1).astype(jnp.float32) / max_exact)
                         / math.log(T5_MAX_DIST / max_exact) * (half - max_exact)).astype(jnp.int32)
    large = jnp.minimum(large, half - 1)
    return jnp.where(rel > 0, half, 0) + jnp.where(n < max_exact, n, large)


def _win_bias(t5_table):
    ring = 4 * BLOCK
    u = jnp.arange(ring)
    rel = jnp.where(u < ring // 2, u, u - ring)
    onehot = (_t5_bucket(rel)[:, None] == jnp.arange(T5_BUCKETS)[None, :]).astype(F32)
    by_rel = jnp.dot(onehot, t5_table.astype(F32) * LOG2E, precision=lax.Precision.HIGHEST)
    by_rel = jnp.where((jnp.abs(rel) <= WINDOW)[:, None], by_rel, NEG).T
    bias = _toeplitz(jnp.roll(by_rel, BLOCK, axis=-1), BLOCK, 3 * BLOCK)
    blk = np.arange(3 * BLOCK) // BLOCK
    cases = [jnp.where((blk != 0)[None, None], bias, NEG), bias, jnp.where((blk != 2)[None, None], bias, NEG)]
    shape = (KVH_B, G_B // WIN_STACK, WIN_STACK * BLOCK, 3 * BLOCK)
    return jnp.stack([jnp.swapaxes(c.reshape(shape), -1, -2) for c in cases]).astype(BF16)


def _layer_constants(qn_a, kn_a, qn_b, kn_b):
    q_scale = LOG2E / math.sqrt(HEAD_DIM)
    tile = lambda gvec, s: jnp.tile(gvec.astype(F32) * s, W_A // HEAD_DIM)
    gains = jnp.stack([tile(qn_a, q_scale), tile(kn_a, 1.0), tile(qn_b, q_scale), tile(kn_b, 1.0)])
    lane = np.arange(LANE_GROUP)
    bd = jnp.asarray(lane[:, None] // HEAD_DIM == lane[None, :] // HEAD_DIM, BF16)
    return gains, bd


def _encoder_layer(x, norm_g, w_in, qn_a, kn_a, rpb_a, qn_b, kn_b, sink_b, w_o_a, w_o_b, w_out, t5_table):
    batch, seq, d = x.shape
    assert d == D_MODEL and seq % GRID_W == 0
    x2d = x.reshape(batch * seq, d)
    gains, bd = _layer_constants(qn_a, kn_a, qn_b, kn_b)
    w_bf = (norm_g.astype(F32)[:, None] * w_in).astype(BF16)
    sink = jnp.repeat(sink_b.astype(F32) * LOG2E, BLOCK).reshape(H_B // WIN_STACK, 1, WIN_STACK * BLOCK)
    y = _layer_call(x2d, w_bf, bd, gains, _na_bias(rpb_a, seq // GRID_W), _win_bias(t5_table), sink,
                    w_o_a.astype(BF16), w_o_b.astype(BF16), w_out.astype(BF16), seq)
    return y.reshape(batch, seq, d)


def kernel(x_prompt, x_sample, norm_g, w_in, qn_a, kn_a, rpb_a, qn_b, kn_b, sink_b, w_o_a, w_o_b, w_out, t5_table):
    y_prompt, y_sample = x_prompt, x_sample
    for l in range(norm_g.shape[0]):
        params = (norm_g[l], w_in[l], qn_a[l], kn_a[l], rpb_a[l], qn_b[l], kn_b[l], sink_b[l],
                  w_o_a[l], w_o_b[l], w_out[l], t5_table)
        y_prompt = _encoder_layer(y_prompt, *params)
        y_sample = _encoder_layer(y_sample, *params)
    return (y_prompt, y_sample)
```

```python
import functools
import math

import jax
import jax.numpy as jnp
import numpy as np
from jax import lax
from jax.experimental import pallas as pl
from jax.experimental.pallas import tpu as pltpu

F32 = jnp.float32
BF16 = jnp.bfloat16
BIAS_DTYPE = BF16

D_MODEL = 1024
HEAD_DIM = 64
H_A = 8
H_B = 8
KVH_B = 2
G_B = H_B // KVH_B
W_A = H_A * HEAD_DIM
W_B = H_B * HEAD_DIM
W_KV_B = KVH_B * HEAD_DIM
GRID_W = 64
NA_ROWS = 8
NA_COLS = 16
WINDOW = 128
BLOCK = 128
T5_BUCKETS = 32
T5_MAX_DIST = 128
EPS = 1e-6
NEG = -1e30
LOG2E = math.log2(math.e)

SUBLANES = 8
ONES_ROWS = 16
LANE_GROUP = 256
HEADS_PER_GROUP = LANE_GROUP // HEAD_DIM
OFF_QA, OFF_KA, OFF_VA, OFF_ZA = 0, W_A, 2 * W_A, 3 * W_A
OFF_QB = 4 * W_A
OFF_KB = OFF_QB + W_B
OFF_VB = OFF_KB + W_KV_B
OFF_ZB = OFF_VB + W_KV_B
OFF_GA = OFF_ZB + W_B
OFF_GB = OFF_GA + D_MODEL
D_IN = OFF_GB + D_MODEL

TQ = 256
ROWS_PER_TILE = TQ // GRID_W
KEY_TILES = 3
STEP_TILES = 1
KV_SLOTS = 2 * STEP_TILES + 1
Q_SLOTS = 2 * STEP_TILES
WIN_STACK = 2
NA_QK_ROWS = 256
WIN_QK_ROWS = 128
LOOKAHEAD = 2
VMEM_LIMIT = 56 * 1024 * 1024


def _const_spec(shape):
    return pl.BlockSpec(shape, lambda *_: (0,) * len(shape))


def _project_tile(x_ref, w_ref, bd_ref, gains_ref,
                  ka_ref, vat_ref, kb_ref, vbt_ref, qat_ref, za_ref, qbt_ref, zb_ref, ga_ref, gb_ref):
    x = x_ref[...]
    inv_rms = lax.rsqrt(jnp.mean(x * x, axis=-1, keepdims=True) + EPS)
    xb = x.astype(BF16)

    def proj(lo, width):
        return jnp.dot(xb, w_ref[:, lo:lo + width], preferred_element_type=F32) * inv_rms

    def head_norm(p, gain):
        width = p.shape[-1]
        sq = (p * p).astype(BF16)
        parts = []
        for lo in range(0, width, LANE_GROUP):
            w = min(LANE_GROUP, width - lo)
            parts.append(jnp.dot(sq[:, lo:lo + w], bd_ref[:w, :w], preferred_element_type=F32))
        ssq = parts[0] if len(parts) == 1 else jnp.concatenate(parts, axis=-1)
        return p * lax.rsqrt(ssq * (1.0 / HEAD_DIM) + EPS) * gain

    def sigmoid(z):
        return 0.5 * jnp.tanh(0.5 * z) + 0.5

    def silu(z):
        return z * sigmoid(z)

    def spread_kv(kv):
        swapped = pltpu.roll(kv, HEAD_DIM, 1)
        low = lax.broadcasted_iota(jnp.int32, kv.shape, 1) < HEAD_DIM
        first = jnp.where(low, kv, swapped).astype(BF16)
        second = jnp.where(low, swapped, kv).astype(BF16)
        return jnp.concatenate([first, first, second, second], axis=-1)

    def store(ref, value, c=None):
        if c is None:
            ref[...] = value().astype(BF16)
        else:
            ref[:, c:c + W_A] = value().astype(BF16)

    early = [
        lambda: store(ka_ref, lambda: head_norm(proj(OFF_KA, W_A), gains_ref[1:2, :])),
        lambda: store(vat_ref, lambda: proj(OFF_VA, W_A).T),
        lambda: store(kb_ref, lambda: spread_kv(head_norm(proj(OFF_KB, W_KV_B), gains_ref[3:4, :W_KV_B]))),
        lambda: store(vbt_ref, lambda: proj(OFF_VB, W_KV_B).T),
    ]
    late = [
        lambda: store(qat_ref, lambda: head_norm(proj(OFF_QA, W_A), gains_ref[0:1, :]).T),
        lambda: store(qbt_ref, lambda: head_norm(proj(OFF_QB, W_B), gains_ref[2:3, :]).T),
        lambda: store(za_ref, lambda: silu(proj(OFF_ZA, W_A))),
        lambda: store(zb_ref, lambda: silu(proj(OFF_ZB, W_B))),
    ]
    for c in range(0, D_MODEL, W_A):
        late.append(lambda c=c: store(ga_ref, lambda: sigmoid(proj(OFF_GA + c, W_A)), c))
        late.append(lambda c=c: store(gb_ref, lambda: sigmoid(proj(OFF_GB + c, W_A)), c))
    return early, late


def _reduce_keys(x, combine, reduce):
    while x.shape[0] % (2 * SUBLANES) == 0:
        half = x.shape[0] // 2
        x = combine(x[:half], x[half:])
    return reduce(x, axis=0, keepdims=True)


def _attend_t(s_t, v_t, sink=None):
    m = _reduce_keys(s_t, jnp.maximum, jnp.max)
    if sink is not None:
        m = jnp.maximum(m, sink)
    p = jnp.exp2((s_t - m).astype(BF16))
    v_ones = jnp.concatenate([v_t, jnp.ones((ONES_ROWS, v_t.shape[1]), BF16)], axis=0)
    o = jnp.dot(v_ones, p, preferred_element_type=F32)
    l = o[HEAD_DIM:HEAD_DIM + 1]
    if sink is not None:
        l = l + jnp.exp2(sink - m)
    return o[:HEAD_DIM] * (1.0 / l)


def _head_rows(qt_ref, g, j, queries):
    lo = g * LANE_GROUP + j * HEAD_DIM
    head = qt_ref[lo:lo + HEAD_DIM, queries]
    above = jnp.zeros((j * HEAD_DIM, head.shape[1]), BF16)
    below = jnp.zeros(((HEADS_PER_GROUP - 1 - j) * HEAD_DIM, head.shape[1]), BF16)
    return jnp.concatenate([part for part in (above, head, below) if part.shape[0]], axis=0)


def _qk_t(keys, qt_masked, chunk):
    parts = [jnp.dot(keys[r:r + chunk], qt_masked, preferred_element_type=F32)
             for r in range(0, keys.shape[0], chunk)]
    return jnp.concatenate(parts, axis=0)


def _na_blocks(qt_ref, k_refs, vt_refs, bias_ref, case):
    blocks = []
    for g in range(W_A // LANE_GROUP):
        cols = slice(g * LANE_GROUP, (g + 1) * LANE_GROUP)
        for j in range(HEADS_PER_GROUP):
            h = g * HEADS_PER_GROUP + j

            def logits(g=g, cols=cols, j=j):
                k4 = jnp.concatenate([r[:, cols] for r in k_refs], axis=0)
                return _qk_t(k4, _head_rows(qt_ref, g, j, slice(None)), NA_QK_ROWS)

            def finish(s_t, h=h):
                feat = slice(h * HEAD_DIM, (h + 1) * HEAD_DIM)
                v_t = jnp.concatenate([r[feat, :] for r in vt_refs], axis=1)
                return _attend_t(s_t + bias_ref[case, h].astype(F32), v_t)

            blocks.append((logits, finish))
    return blocks


def _win_blocks(qt_ref, k_refs, vt_refs, bias_ref, sink_ref, t, nt):
    n_sub = TQ // BLOCK
    blocks = []
    for g in range(KVH_B):
        cols = slice(g * LANE_GROUP, (g + 1) * LANE_GROUP)
        for sub in range(n_sub):
            rows = slice(sub * BLOCK, (sub + 1) * BLOCK)
            key_lo = TQ + (sub - 1) * BLOCK
            if sub == 0:
                case = jnp.where(t == 0, 0, 1)
            elif sub == n_sub - 1:
                case = jnp.where(t == nt - 1, 2, 1)
            else:
                case = 1
            for pair in range(G_B // WIN_STACK):

                def logits(g=g, cols=cols, rows=rows, key_lo=key_lo, pair=pair):
                    kwin = jnp.concatenate([r[:, cols] for r in k_refs], axis=0)
                    heads = range(pair * WIN_STACK, (pair + 1) * WIN_STACK)
                    qt_pair = jnp.concatenate([_head_rows(qt_ref, g, j, rows) for j in heads], axis=1)
                    return _qk_t(kwin[key_lo:key_lo + 3 * BLOCK], qt_pair, WIN_QK_ROWS)

                def finish(s_t, g=g, key_lo=key_lo, pair=pair, case=case):
                    vwin_t = jnp.concatenate([r[g * HEAD_DIM:(g + 1) * HEAD_DIM, :] for r in vt_refs], axis=1)
                    v3_t = vwin_t[:, key_lo:key_lo + 3 * BLOCK]
                    return _attend_t(s_t + bias_ref[case, g, pair].astype(F32), v3_t,
                                     sink_ref[g * (G_B // WIN_STACK) + pair])

                blocks.append((logits, finish))
    return blocks


def _run_interleaved(blocks, fillers):
    outs, pending = [], []
    fillers = list(fillers)
    every = max(1, len(blocks) // max(1, len(fillers)))
    for i, (logits, finish) in enumerate(blocks):
        pending.append((logits(), finish))
        if len(pending) > LOOKAHEAD:
            s_t, fin = pending.pop(0)
            outs.append(fin(s_t))
        if fillers and i % every == every - 1:
            fillers.pop(0)()
    outs += [fin(s_t) for s_t, fin in pending]
    for filler in fillers:
        filler()
    return outs


def _assemble_win(outs):
    n_sub, n_pair = TQ // BLOCK, G_B // WIN_STACK
    feats = []
    for g in range(KVH_B):
        per_sub = []
        for sub in range(n_sub):
            pairs = outs[(g * n_sub + sub) * n_pair:(g * n_sub + sub + 1) * n_pair]
            heads = [o[:, i * BLOCK:(i + 1) * BLOCK] for o in pairs for i in range(WIN_STACK)]
            per_sub.append(jnp.concatenate(heads, axis=0))
        feats.append(jnp.concatenate(per_sub, axis=1))
    return jnp.concatenate(feats, axis=0)


def _layer_kernel(xp_ref, xr_ref, w_ref, bd_ref, gains_ref, bias_a_ref, bias_b_ref, sink_ref,
                  woa_ref, wob_ref, wout_ref, y_ref,
                  ka_ring, vat_ring, kb_ring, vbt_ring, qat_buf, za_buf, qbt_buf, zb_buf, ga_buf, gb_buf, *, nt):
    s = pl.program_id(0)
    kv_rings = (ka_ring, vat_ring, kb_ring, vbt_ring)
    q_bufs = (qat_buf, za_buf, qbt_buf, zb_buf, ga_buf, gb_buf)

    @pl.when(s == 0)
    def _():
        for buf in kv_rings + q_bufs:
            buf[...] = jnp.zeros(buf.shape, buf.dtype)

    kv_slot = lambda tile: lax.rem(tile + 2 * KV_SLOTS, KV_SLOTS)
    q_slot = lambda tile: lax.rem(tile + 2 * Q_SLOTS, Q_SLOTS)

    fillers = []
    for u in range(STEP_TILES):
        tile = s * STEP_TILES + u
        early, late = _project_tile(
            xp_ref.at[u * TQ:(u + 1) * TQ, :], w_ref, bd_ref, gains_ref,
            *[ring.at[kv_slot(tile)] for ring in kv_rings],
            qat_buf.at[q_slot(tile)], za_buf.at[q_slot(tile)], qbt_buf.at[q_slot(tile)], zb_buf.at[q_slot(tile)],
            ga_buf.at[q_slot(tile)], gb_buf.at[q_slot(tile)])
        if u == 0:
            for chunk in early:
                chunk()
            fillers += late
        else:
            fillers += early + late

    blocks, attended = [], []
    for u in range(STEP_TILES):
        tile = (s - 1) * STEP_TILES + u
        t = lax.rem(jnp.maximum(tile, 0), nt)
        case = jnp.where(t == 0, 0, jnp.where(t == nt - 1, 2, 1))
        window = [kv_slot(tile + d) for d in (-1, 0, 1)]
        na = _na_blocks(qat_buf.at[q_slot(tile)], [ka_ring.at[i] for i in window],
                        [vat_ring.at[i] for i in window], bias_a_ref, case)
        win = _win_blocks(qbt_buf.at[q_slot(tile)], [kb_ring.at[i] for i in window],
                          [vbt_ring.at[i] for i in window], bias_b_ref, sink_ref, t, nt)
        attended.append((q_slot(tile), len(blocks), len(na), len(win)))
        blocks += na + win
    outs = _run_interleaved(blocks, fillers)

    oa, ob, ga, gb = [], [], [], []
    for slot, lo, n_na, n_win in attended:
        oa.append((jnp.concatenate(outs[lo:lo + n_na], axis=0).T * za_buf[slot].astype(F32)).astype(BF16))
        ob.append((_assemble_win(outs[lo + n_na:lo + n_na + n_win]).T * zb_buf[slot].astype(F32)).astype(BF16))
        ga.append(ga_buf[slot].astype(F32))
        gb.append(gb_buf[slot].astype(F32))
    rows = lambda parts: jnp.concatenate(parts, axis=0)
    a = jnp.dot(rows(oa), woa_ref[...], preferred_element_type=F32)
    b = jnp.dot(rows(ob), wob_ref[...], preferred_element_type=F32)
    merged = rows(ga) * a + rows(gb) * b
    y_ref[...] = xr_ref[...] + jnp.dot(merged.astype(BF16), wout_ref[...], preferred_element_type=F32)


def _layer_call(x2d, w_bf, bd, gains, bias_a, bias_b, sink, woa, wob, wout, seq):
    n_steps = x2d.shape[0] // (TQ * STEP_TILES)
    nt = seq // TQ
    assert seq % TQ == 0 and nt >= 3 and x2d.shape[0] % (TQ * STEP_TILES) == 0
    proj_step = lambda s: (jnp.minimum(s, n_steps - 1), 0)
    attn_step = lambda s: (jnp.maximum(s - 1, 0), 0)
    consts = (w_bf, bd, gains, bias_a, bias_b, sink, woa, wob, wout)
    ring = lambda slots, rows, cols: pltpu.VMEM((slots, rows, cols), BF16)
    block = (TQ * STEP_TILES, D_MODEL)
    return pl.pallas_call(
        functools.partial(_layer_kernel, nt=nt),
        grid=(n_steps + 1,),
        in_specs=[pl.BlockSpec(block, proj_step), pl.BlockSpec(block, attn_step)]
                 + [_const_spec(c.shape) for c in consts],
        out_specs=pl.BlockSpec(block, attn_step),
        out_shape=jax.ShapeDtypeStruct(x2d.shape, F32),
        scratch_shapes=[ring(KV_SLOTS, TQ, W_A), ring(KV_SLOTS, W_A, TQ), ring(KV_SLOTS, TQ, W_B),
                        ring(KV_SLOTS, W_KV_B, TQ), ring(Q_SLOTS, W_A, TQ), ring(Q_SLOTS, TQ, W_A),
                        ring(Q_SLOTS, W_B, TQ), ring(Q_SLOTS, TQ, W_B), ring(Q_SLOTS, TQ, D_MODEL),
                        ring(Q_SLOTS, TQ, D_MODEL)],
        compiler_params=pltpu.CompilerParams(dimension_semantics=("arbitrary",),
                                             vmem_limit_bytes=VMEM_LIMIT),
        name="encoder_layer",
    )(x2d, x2d, *consts)


def _toeplitz(v, m, n):
    w = v.shape[-1]
    assert n <= w - 1
    flat = jnp.tile(v, m)[..., :m * (w - 1)]
    return flat.reshape(v.shape[:-1] + (m, w - 1))[..., :n]


def _na_bias(rpb, rows):
    kr = min(NA_ROWS, rows)
    assert kr == NA_ROWS and rows >= 3 * ROWS_PER_TILE
    n_key_rows = KEY_TILES * ROWS_PER_TILE
    rpb = rpb.astype(F32) * LOG2E
    ring_pad = jnp.zeros(rpb.shape[:-1] + (2 * GRID_W - (2 * NA_COLS - 1),), F32)
    ring = jnp.concatenate([rpb[..., NA_COLS - 1:], ring_pad, rpb[..., :NA_COLS - 1]], axis=-1)
    col = _toeplitz(ring, GRID_W, GRID_W).astype(BIAS_DTYPE)
    first = NA_ROWS - 1 - ROWS_PER_TILE
    per_row = [col[:, first - i:first - i + n_key_rows] for i in range(ROWS_PER_TILE)]
    vals_t = jnp.transpose(jnp.stack(per_row, axis=1), (0, 2, 4, 1, 3)).reshape(H_A, KEY_TILES * TQ, TQ)
    i = np.arange(ROWS_PER_TILE)[:, None, None, None]
    qc = np.arange(GRID_W)[None, :, None, None]
    j = np.arange(n_key_rows)[None, None, :, None]
    kc = np.arange(GRID_W)[None, None, None, :]
    cs = np.clip(qc - NA_COLS // 2, 0, GRID_W - NA_COLS)
    col_ok = (kc >= cs) & (kc < cs + NA_COLS)
    shape = (ROWS_PER_TILE, GRID_W, n_key_rows, GRID_W)
    row_ok = [
        (j >= ROWS_PER_TILE) & (j < ROWS_PER_TILE + kr),
        (j - i >= 0) & (j - i < kr),
        (j >= 0) & (j < kr),
    ]
    out = []
    for ok in row_ok:
        valid_t = np.broadcast_to(ok & col_ok, shape).transpose(2, 3, 0, 1).reshape(KEY_TILES * TQ, TQ)
        out.append(jnp.where(valid_t[None], vals_t, jnp.asarray(NEG, BIAS_DTYPE)))
    return jnp.stack(out)


def _t5_bucket(rel):
    half = T5_BUCKETS // 2
    max_exact = half // 2
    n = jnp.abs(rel)
    large = max_exact + (jnp.log(jnp.maximum(n, 1).astype(jnp.float32) / max_exact)
                         / math.log(T5_MAX_DIST / max_exact) * (half - max_exact)).astype(jnp.int32)
    large = jnp.minimum(large, half - 1)
    return jnp.where(rel > 0, half, 0) + jnp.where(n < max_exact, n, large)


def _win_bias(t5_table):
    ring = 4 * BLOCK
    u = jnp.arange(ring)
    rel = jnp.where(u < ring // 2, u, u - ring)
    onehot = (_t5_bucket(rel)[:, None] == jnp.arange(T5_BUCKETS)[None, :]).astype(F32)
    by_rel = jnp.dot(onehot, t5_table.astype(F32) * LOG2E, precision=lax.Precision.HIGHEST)
    by_rel = jnp.where((jnp.abs(rel) <= WINDOW)[:, None], by_rel, NEG).T
    bias = _toeplitz(jnp.roll(by_rel, BLOCK, axis=-1), BLOCK, 3 * BLOCK)
    blk = np.arange(3 * BLOCK) // BLOCK
    cases = [jnp.where((blk != 0)[None, None], bias, NEG), bias, jnp.where((blk != 2)[None, None], bias, NEG)]
    shape = (KVH_B, G_B // WIN_STACK, WIN_STACK * BLOCK, 3 * BLOCK)
    return jnp.stack([jnp.swapaxes(c.reshape(shape), -1, -2) for c in cases]).astype(BIAS_DTYPE)


def _layer_constants(qn_a, kn_a, qn_b, kn_b):
    q_scale = LOG2E / math.sqrt(HEAD_DIM)
    tile = lambda gvec, s: jnp.tile(gvec.astype(F32) * s, W_A // HEAD_DIM)
    gains = jnp.stack([tile(qn_a, q_scale), tile(kn_a, 1.0), tile(qn_b, q_scale), tile(kn_b, 1.0)])
    lane = np.arange(LANE_GROUP)
    bd = jnp.asarray(lane[:, None] // HEAD_DIM == lane[None, :] // HEAD_DIM, BF16)
    return gains, bd


def _encoder_layer(x, norm_g, w_in, qn_a, kn_a, rpb_a, qn_b, kn_b, sink_b, w_o_a, w_o_b, w_out, t5_table):
    batch, seq, d = x.shape
    assert d == D_MODEL and seq % GRID_W == 0
    x2d = x.reshape(batch * seq, d)
    gains, bd = _layer_constants(qn_a, kn_a, qn_b, kn_b)
    w_bf = (norm_g.astype(F32)[:, None] * w_in).astype(BF16)
    sink = jnp.repeat(sink_b.astype(F32) * LOG2E, BLOCK).reshape(H_B // WIN_STACK, 1, WIN_STACK * BLOCK)
    y = _layer_call(x2d, w_bf, bd, gains, _na_bias(rpb_a, seq // GRID_W), _win_bias(t5_table), sink,
                    w_o_a.astype(BF16), w_o_b.astype(BF16), w_out.astype(BF16), seq)
    return y.reshape(batch, seq, d)


def kernel(x_prompt, x_sample, norm_g, w_in, qn_a, kn_a, rpb_a, qn_b, kn_b, sink_b, w_o_a, w_o_b, w_out, t5_table):
    y_prompt, y_sample = x_prompt, x_sample
    for l in range(norm_g.shape[0]):
        params = (norm_g[l], w_in[l], qn_a[l], kn_a[l], rpb_a[l], qn_b[l], kn_b[l], sink_b[l],
                  w_o_a[l], w_o_b[l], w_out[l], t5_table)
        y_prompt = _encoder_layer(y_prompt, *params)
        y_sample = _encoder_layer(y_sample, *params)
    return (y_prompt, y_sample)
```

```python
import functools
import math

import jax
import jax.numpy as jnp
import numpy as np
from jax import lax
from jax.experimental import pallas as pl
from jax.experimental.pallas import tpu as pltpu

F32 = jnp.float32
BF16 = jnp.bfloat16
BIAS_DTYPE = BF16

D_MODEL = 1024
HEAD_DIM = 64
H_A = 8
H_B = 8
KVH_B = 2
G_B = H_B // KVH_B
W_A = H_A * HEAD_DIM
W_B = H_B * HEAD_DIM
W_KV_B = KVH_B * HEAD_DIM
GRID_W = 64
NA_ROWS = 8
NA_COLS = 16
WINDOW = 128
BLOCK = 128
T5_BUCKETS = 32
T5_MAX_DIST = 128
EPS = 1e-6
NEG = -1e30
LOG2E = math.log2(math.e)

SUBLANES = 8
ONES_ROWS = 16
LANE_GROUP = 256
HEADS_PER_GROUP = LANE_GROUP // HEAD_DIM
OFF_QA, OFF_KA, OFF_VA, OFF_ZA = 0, W_A, 2 * W_A, 3 * W_A
OFF_QB = 4 * W_A
OFF_KB = OFF_QB + W_B
OFF_VB = OFF_KB + W_KV_B
OFF_ZB = OFF_VB + W_KV_B
OFF_GA = OFF_ZB + W_B
OFF_GB = OFF_GA + D_MODEL
D_IN = OFF_GB + D_MODEL

TQ = 256
ROWS_PER_TILE = TQ // GRID_W
KEY_TILES = 3
STEP_TILES = 1
KV_SLOTS = 2 * STEP_TILES + 1
Q_SLOTS = 2 * STEP_TILES
WIN_STACK = 2
NA_QK_ROWS = 256
WIN_QK_ROWS = 128
LOOKAHEAD = 2
VMEM_LIMIT = 56 * 1024 * 1024


def _const_spec(shape):
    return pl.BlockSpec(shape, lambda *_: (0,) * len(shape))


def _project_tile(x_ref, w_ref, bd_ref, gains_ref,
                  ka_ref, vat_ref, kb_ref, vbt_ref, qat_ref, za_ref, qbt_ref, zb_ref, ga_ref, gb_ref):
    x = x_ref[...]
    inv_rms = lax.rsqrt(jnp.mean(x * x, axis=-1, keepdims=True) + EPS)
    xb = x.astype(BF16)

    def proj(lo, width):
        return jnp.dot(xb, w_ref[:, lo:lo + width], preferred_element_type=F32) * inv_rms

    def head_norm(p, gain):
        width = p.shape[-1]
        sq = (p * p).astype(BF16)
        parts = []
        for lo in range(0, width, LANE_GROUP):
            w = min(LANE_GROUP, width - lo)
            parts.append(jnp.dot(sq[:, lo:lo + w], bd_ref[:w, :w], preferred_element_type=F32))
        ssq = parts[0] if len(parts) == 1 else jnp.concatenate(parts, axis=-1)
        return p * lax.rsqrt(ssq * (1.0 / HEAD_DIM) + EPS) * gain

    def head_norm_t(p):
        p_t = p.T
        heads = []
        for lo in range(0, p_t.shape[0], HEAD_DIM):
            head = p_t[lo:lo + HEAD_DIM]
            ms = jnp.sum(head * head, axis=0, keepdims=True) * (1.0 / HEAD_DIM)
            heads.append(head * lax.rsqrt(ms + EPS))
        return jnp.concatenate(heads, axis=0)

    def sigmoid(z):
        return 0.5 * jnp.tanh(0.5 * z) + 0.5

    def silu(z):
        return z * sigmoid(z)

    def spread_kv(kv):
        swapped = pltpu.roll(kv, HEAD_DIM, 1)
        low = lax.broadcasted_iota(jnp.int32, kv.shape, 1) < HEAD_DIM
        first = jnp.where(low, kv, swapped).astype(BF16)
        second = jnp.where(low, swapped, kv).astype(BF16)
        return jnp.concatenate([first, first, second, second], axis=-1)

    def store(ref, value, c=None):
        if c is None:
            ref[...] = value().astype(BF16)
        else:
            ref[:, c:c + W_A] = value().astype(BF16)

    early = [
        lambda: store(ka_ref, lambda: head_norm_t(proj(OFF_KA, W_A)).T * gains_ref[0:1, :]),
        lambda: store(vat_ref, lambda: proj(OFF_VA, W_A).T),
        lambda: store(kb_ref, lambda: spread_kv(head_norm(proj(OFF_KB, W_KV_B), gains_ref[1:2, :W_KV_B]))),
        lambda: store(vbt_ref, lambda: proj(OFF_VB, W_KV_B).T),
    ]
    late = [
        lambda: store(qat_ref, lambda: head_norm_t(proj(OFF_QA, W_A))),
        lambda: store(qbt_ref, lambda: head_norm_t(proj(OFF_QB, W_B))),
        lambda: store(za_ref, lambda: silu(proj(OFF_ZA, W_A))),
        lambda: store(zb_ref, lambda: silu(proj(OFF_ZB, W_B))),
    ]
    for c in range(0, D_MODEL, W_A):
        late.append(lambda c=c: store(ga_ref, lambda: sigmoid(proj(OFF_GA + c, W_A)), c))
        late.append(lambda c=c: store(gb_ref, lambda: sigmoid(proj(OFF_GB + c, W_A)), c))
    return early, late


def _reduce_keys(x, combine, reduce):
    while x.shape[0] % (2 * SUBLANES) == 0:
        half = x.shape[0] // 2
        x = combine(x[:half], x[half:])
    return reduce(x, axis=0, keepdims=True)


def _attend_t(s_t, v_t, sink=None):
    m = _reduce_keys(s_t, jnp.maximum, jnp.max)
    if sink is not None:
        m = jnp.maximum(m, sink)
    p = jnp.exp2((s_t - m).astype(BF16))
    v_ones = jnp.concatenate([v_t, jnp.ones((ONES_ROWS, v_t.shape[1]), BF16)], axis=0)
    o = jnp.dot(v_ones, p, preferred_element_type=F32)
    l = o[HEAD_DIM:HEAD_DIM + 1]
    if sink is not None:
        l = l + jnp.exp2(sink - m)
    return o[:HEAD_DIM] * (1.0 / l)


def _head_rows(qt_ref, g, j, queries):
    lo = g * LANE_GROUP + j * HEAD_DIM
    head = qt_ref[lo:lo + HEAD_DIM, queries]
    above = jnp.zeros((j * HEAD_DIM, head.shape[1]), BF16)
    below = jnp.zeros(((HEADS_PER_GROUP - 1 - j) * HEAD_DIM, head.shape[1]), BF16)
    return jnp.concatenate([part for part in (above, head, below) if part.shape[0]], axis=0)


def _qk_t(keys, qt_masked, chunk):
    parts = [jnp.dot(keys[r:r + chunk], qt_masked, preferred_element_type=F32)
             for r in range(0, keys.shape[0], chunk)]
    return jnp.concatenate(parts, axis=0)


def _na_blocks(qt_ref, k_refs, vt_refs, bias_ref, case):
    blocks = []
    for g in range(W_A // LANE_GROUP):
        cols = slice(g * LANE_GROUP, (g + 1) * LANE_GROUP)
        for j in range(HEADS_PER_GROUP):
            h = g * HEADS_PER_GROUP + j

            def logits(g=g, cols=cols, j=j):
                k4 = jnp.concatenate([r[:, cols] for r in k_refs], axis=0)
                return _qk_t(k4, _head_rows(qt_ref, g, j, slice(None)), NA_QK_ROWS)

            def finish(s_t, h=h):
                feat = slice(h * HEAD_DIM, (h + 1) * HEAD_DIM)
                v_t = jnp.concatenate([r[feat, :] for r in vt_refs], axis=1)
                return _attend_t(s_t + bias_ref[case, h].astype(F32), v_t)

            blocks.append((logits, finish))
    return blocks


def _win_blocks(qt_ref, k_refs, vt_refs, bias_ref, sink_ref, t, nt):
    n_sub = TQ // BLOCK
    blocks = []
    for g in range(KVH_B):
        cols = slice(g * LANE_GROUP, (g + 1) * LANE_GROUP)
        for sub in range(n_sub):
            rows = slice(sub * BLOCK, (sub + 1) * BLOCK)
            key_lo = TQ + (sub - 1) * BLOCK
            if sub == 0:
                case = jnp.where(t == 0, 0, 1)
            elif sub == n_sub - 1:
                case = jnp.where(t == nt - 1, 2, 1)
            else:
                case = 1
            for pair in range(G_B // WIN_STACK):

                def logits(g=g, cols=cols, rows=rows, key_lo=key_lo, pair=pair):
                    kwin = jnp.concatenate([r[:, cols] for r in k_refs], axis=0)
                    heads = range(pair * WIN_STACK, (pair + 1) * WIN_STACK)
                    qt_pair = jnp.concatenate([_head_rows(qt_ref, g, j, rows) for j in heads], axis=1)
                    return _qk_t(kwin[key_lo:key_lo + 3 * BLOCK], qt_pair, WIN_QK_ROWS)

                def finish(s_t, g=g, key_lo=key_lo, pair=pair, case=case):
                    vwin_t = jnp.concatenate([r[g * HEAD_DIM:(g + 1) * HEAD_DIM, :] for r in vt_refs], axis=1)
                    v3_t = vwin_t[:, key_lo:key_lo + 3 * BLOCK]
                    return _attend_t(s_t + bias_ref[case, g, pair].astype(F32), v3_t,
                                     sink_ref[g * (G_B // WIN_STACK) + pair])

                blocks.append((logits, finish))
    return blocks


def _run_interleaved(blocks, fillers):
    outs, pending = [], []
    fillers = list(fillers)
    every = max(1, len(blocks) // max(1, len(fillers)))
    for i, (logits, finish) in enumerate(blocks):
        pending.append((logits(), finish))
        if len(pending) > LOOKAHEAD:
            s_t, fin = pending.pop(0)
            outs.append(fin(s_t))
        if fillers and i % every == every - 1:
            fillers.pop(0)()
    outs += [fin(s_t) for s_t, fin in pending]
    for filler in fillers:
        filler()
    return outs


def _assemble_win(outs):
    n_sub, n_pair = TQ // BLOCK, G_B // WIN_STACK
    feats = []
    for g in range(KVH_B):
        per_sub = []
        for sub in range(n_sub):
            pairs = outs[(g * n_sub + sub) * n_pair:(g * n_sub + sub + 1) * n_pair]
            heads = [o[:, i * BLOCK:(i + 1) * BLOCK] for o in pairs for i in range(WIN_STACK)]
            per_sub.append(jnp.concatenate(heads, axis=0))
        feats.append(jnp.concatenate(per_sub, axis=1))
    return jnp.concatenate(feats, axis=0)


def _layer_kernel(xp_ref, xr_ref, w_ref, bd_ref, gains_ref, bias_a_ref, bias_b_ref, sink_ref,
                  woa_ref, wob_ref, wout_ref, y_ref,
                  ka_ring, vat_ring, kb_ring, vbt_ring, qat_buf, za_buf, qbt_buf, zb_buf, ga_buf, gb_buf, *, nt):
    s = pl.program_id(0)
    kv_rings = (ka_ring, vat_ring, kb_ring, vbt_ring)
    q_bufs = (qat_buf, za_buf, qbt_buf, zb_buf, ga_buf, gb_buf)

    @pl.when(s == 0)
    def _():
        for buf in kv_rings + q_bufs:
            buf[...] = jnp.zeros(buf.shape, buf.dtype)

    kv_slot = lambda tile: lax.rem(tile + 2 * KV_SLOTS, KV_SLOTS)
    q_slot = lambda tile: lax.rem(tile + 2 * Q_SLOTS, Q_SLOTS)

    fillers = []
    for u in range(STEP_TILES):
        tile = s * STEP_TILES + u
        early, late = _project_tile(
            xp_ref.at[u * TQ:(u + 1) * TQ, :], w_ref, bd_ref, gains_ref,
            *[ring.at[kv_slot(tile)] for ring in kv_rings],
            qat_buf.at[q_slot(tile)], za_buf.at[q_slot(tile)], qbt_buf.at[q_slot(tile)], zb_buf.at[q_slot(tile)],
            ga_buf.at[q_slot(tile)], gb_buf.at[q_slot(tile)])
        if u == 0:
            for chunk in early:
                chunk()
            fillers += late
        else:
            fillers += early + late

    blocks, attended = [], []
    for u in range(STEP_TILES):
        tile = (s - 1) * STEP_TILES + u
        t = lax.rem(jnp.maximum(tile, 0), nt)
        case = jnp.where(t == 0, 0, jnp.where(t == nt - 1, 2, 1))
        window = [kv_slot(tile + d) for d in (-1, 0, 1)]
        na = _na_blocks(qat_buf.at[q_slot(tile)], [ka_ring.at[i] for i in window],
                        [vat_ring.at[i] for i in window], bias_a_ref, case)
        win = _win_blocks(qbt_buf.at[q_slot(tile)], [kb_ring.at[i] for i in window],
                          [vbt_ring.at[i] for i in window], bias_b_ref, sink_ref, t, nt)
        attended.append((q_slot(tile), len(blocks), len(na), len(win)))
        blocks += na + win
    outs = _run_interleaved(blocks, fillers)

    oa, ob, ga, gb = [], [], [], []
    for slot, lo, n_na, n_win in attended:
        oa.append((jnp.concatenate(outs[lo:lo + n_na], axis=0).T * za_buf[slot].astype(F32)).astype(BF16))
        ob.append((_assemble_win(outs[lo + n_na:lo + n_na + n_win]).T * zb_buf[slot].astype(F32)).astype(BF16))
        ga.append(ga_buf[slot].astype(F32))
        gb.append(gb_buf[slot].astype(F32))
    rows = lambda parts: jnp.concatenate(parts, axis=0)
    a = jnp.dot(rows(oa), woa_ref[...], preferred_element_type=F32)
    b = jnp.dot(rows(ob), wob_ref[...], preferred_element_type=F32)
    merged = rows(ga) * a + rows(gb) * b
    y_ref[...] = xr_ref[...] + jnp.dot(merged.astype(BF16), wout_ref[...], preferred_element_type=F32)


def _layer_call(x2d, w_bf, bd, gains, bias_a, bias_b, sink, woa, wob, wout, seq):
    n_steps = x2d.shape[0] // (TQ * STEP_TILES)
    nt = seq // TQ
    assert seq % TQ == 0 and nt >= 3 and x2d.shape[0] % (TQ * STEP_TILES) == 0
    proj_step = lambda s: (jnp.minimum(s, n_steps - 1), 0)
    attn_step = lambda s: (jnp.maximum(s - 1, 0), 0)
    consts = (w_bf, bd, gains, bias_a, bias_b, sink, woa, wob, wout)
    ring = lambda slots, rows, cols: pltpu.VMEM((slots, rows, cols), BF16)
    block = (TQ * STEP_TILES, D_MODEL)
    return pl.pallas_call(
        functools.partial(_layer_kernel, nt=nt),
        grid=(n_steps + 1,),
        in_specs=[pl.BlockSpec(block, proj_step), pl.BlockSpec(block, attn_step)]
                 + [_const_spec(c.shape) for c in consts],
        out_specs=pl.BlockSpec(block, attn_step),
        out_shape=jax.ShapeDtypeStruct(x2d.shape, F32),
        scratch_shapes=[ring(KV_SLOTS, TQ, W_A), ring(KV_SLOTS, W_A, TQ), ring(KV_SLOTS, TQ, W_B),
                        ring(KV_SLOTS, W_KV_B, TQ), ring(Q_SLOTS, W_A, TQ), ring(Q_SLOTS, TQ, W_A),
                        ring(Q_SLOTS, W_B, TQ), ring(Q_SLOTS, TQ, W_B), ring(Q_SLOTS, TQ, D_MODEL),
                        ring(Q_SLOTS, TQ, D_MODEL)],
        compiler_params=pltpu.CompilerParams(dimension_semantics=("arbitrary",),
                                             vmem_limit_bytes=VMEM_LIMIT),
        name="encoder_layer",
    )(x2d, x2d, *consts)


def _toeplitz(v, m, n):
    w = v.shape[-1]
    assert m <= w and n <= w
    doubled = jnp.concatenate([v, v], axis=-1)
    return jnp.stack([doubled[..., w - q:w - q + n] for q in range(m)], axis=-2)


def _na_bias(rpb, rows):
    kr = min(NA_ROWS, rows)
    assert kr == NA_ROWS and rows >= 3 * ROWS_PER_TILE
    n_key_rows = KEY_TILES * ROWS_PER_TILE
    rpb = rpb.astype(F32) * LOG2E
    ring_pad = jnp.zeros(rpb.shape[:-1] + (2 * GRID_W - (2 * NA_COLS - 1),), F32)
    ring = jnp.concatenate([rpb[..., NA_COLS - 1:], ring_pad, rpb[..., :NA_COLS - 1]], axis=-1)
    col = _toeplitz(ring, GRID_W, GRID_W).astype(BIAS_DTYPE)
    first = NA_ROWS - 1 - ROWS_PER_TILE
    per_row = [col[:, first - i:first - i + n_key_rows] for i in range(ROWS_PER_TILE)]
    vals_t = jnp.transpose(jnp.stack(per_row, axis=1), (0, 2, 4, 1, 3)).reshape(H_A, KEY_TILES * TQ, TQ)
    i = np.arange(ROWS_PER_TILE)[:, None, None, None]
    qc = np.arange(GRID_W)[None, :, None, None]
    j = np.arange(n_key_rows)[None, None, :, None]
    kc = np.arange(GRID_W)[None, None, None, :]
    cs = np.clip(qc - NA_COLS // 2, 0, GRID_W - NA_COLS)
    col_ok = (kc >= cs) & (kc < cs + NA_COLS)
    shape = (ROWS_PER_TILE, GRID_W, n_key_rows, GRID_W)
    row_ok = [
        (j >= ROWS_PER_TILE) & (j < ROWS_PER_TILE + kr),
        (j - i >= 0) & (j - i < kr),
        (j >= 0) & (j < kr),
    ]
    out = []
    for ok in row_ok:
        valid_t = np.broadcast_to(ok & col_ok, shape).transpose(2, 3, 0, 1).reshape(KEY_TILES * TQ, TQ)
        out.append(jnp.where(valid_t[None], vals_t, jnp.asarray(NEG, BIAS_DTYPE)))
    return jnp.stack(out)


def _t5_bucket(rel):
    half = T5_BUCKETS // 2
    max_exact = half // 2
    n = jnp.abs(rel)
    large = max_exact + (jnp.log(jnp.maximum(n, 1).astype(jnp.float32) / max_exact)
                         / math.log(T5_MAX_DIST / max_exact) * (half - max_exact)).astype(jnp.int32)
    large = jnp.minimum(large, half - 1)
    return jnp.where(rel > 0, half, 0) + jnp.where(n < max_exact, n, large)


def _win_bias(t5_table):
    ring = 4 * BLOCK
    u = jnp.arange(ring)
    rel = jnp.where(u < ring // 2, u, u - ring)
    onehot = (_t5_bucket(rel)[:, None] == jnp.arange(T5_BUCKETS)[None, :]).astype(F32)
    by_rel = jnp.dot(onehot, t5_table.astype(F32) * LOG2E, precision=lax.Precision.HIGHEST)
    by_rel = jnp.where((jnp.abs(rel) <= WINDOW)[:, None], by_rel, NEG).T
    bias = _toeplitz(jnp.roll(by_rel, BLOCK, axis=-1), BLOCK, 3 * BLOCK)
    blk = np.arange(3 * BLOCK) // BLOCK
    cases = [jnp.where((blk != 0)[None, None], bias, NEG), bias, jnp.where((blk != 2)[None, None], bias, NEG)]
    shape = (KVH_B, G_B // WIN_STACK, WIN_STACK * BLOCK, 3 * BLOCK)
    return jnp.stack([jnp.swapaxes(c.reshape(shape), -1, -2) for c in cases]).astype(BIAS_DTYPE)


def _layer_constants(qn_a, kn_a, qn_b, kn_b):
    q_scale = LOG2E / math.sqrt(HEAD_DIM)
    tile = lambda gvec: jnp.tile(gvec.astype(F32), W_A // HEAD_DIM)
    gains = jnp.stack([tile(qn_a) * tile(kn_a) * q_scale, tile(qn_b) * tile(kn_b) * q_scale])
    lane = np.arange(LANE_GROUP)
    bd = jnp.asarray(lane[:, None] // HEAD_DIM == lane[None, :] // HEAD_DIM, BF16)
    return gains, bd


def _encoder_layer(x, norm_g, w_in, qn_a, kn_a, rpb_a, qn_b, kn_b, sink_b, w_o_a, w_o_b, w_out, t5_table):
    batch, seq, d = x.shape
    assert d == D_MODEL and seq % GRID_W == 0
    x2d = x.reshape(batch * seq, d)
    gains, bd = _layer_constants(qn_a, kn_a, qn_b, kn_b)
    w_bf = (norm_g.astype(F32)[:, None] * w_in).astype(BF16)
    sink = jnp.repeat(sink_b.astype(F32) * LOG2E, BLOCK).reshape(H_B // WIN_STACK, 1, WIN_STACK * BLOCK)
    y = _layer_call(x2d, w_bf, bd, gains, _na_bias(rpb_a, seq // GRID_W), _win_bias(t5_table), sink,
                    w_o_a.astype(BF16), w_o_b.astype(BF16), w_out.astype(BF16), seq)
    return y.reshape(batch, seq, d)


def kernel(x_prompt, x_sample, norm_g, w_in, qn_a, kn_a, rpb_a, qn_b, kn_b, sink_b, w_o_a, w_o_b, w_out, t5_table):
    y_prompt, y_sample = x_prompt, x_sample
    for l in range(norm_g.shape[0]):
        params = (norm_g[l], w_in[l], qn_a[l], kn_a[l], rpb_a[l], qn_b[l], kn_b[l], sink_b[l],
                  w_o_a[l], w_o_b[l], w_out[l], t5_table)
        y_prompt = _encoder_layer(y_prompt, *params)
        y_sample = _encoder_layer(y_sample, *params)
    return (y_prompt, y_sample)
```

```python
import functools
import math

import jax
import jax.numpy as jnp
import numpy as np
from jax import lax
from jax.experimental import pallas as pl
from jax.experimental.pallas import tpu as pltpu

F32 = jnp.float32
BF16 = jnp.bfloat16
BIAS_DTYPE = BF16

D_MODEL = 1024
HEAD_DIM = 64
H_A = 8
H_B = 8
KVH_B = 2
G_B = H_B // KVH_B
W_A = H_A * HEAD_DIM
W_B = H_B * HEAD_DIM
W_KV_B = KVH_B * HEAD_DIM
GRID_W = 64
NA_ROWS = 8
NA_COLS = 16
WINDOW = 128
BLOCK = 128
T5_BUCKETS = 32
T5_MAX_DIST = 128
EPS = 1e-6
NEG = -1e30
LOG2E = math.log2(math.e)

SUBLANES = 8
ONES_ROWS = 16
LANE_GROUP = 256
HEADS_PER_GROUP = LANE_GROUP // HEAD_DIM
OFF_QA, OFF_KA, OFF_VA, OFF_ZA = 0, W_A, 2 * W_A, 3 * W_A
OFF_QB = 4 * W_A
OFF_KB = OFF_QB + W_B
OFF_VB = OFF_KB + W_KV_B
OFF_ZB = OFF_VB + W_KV_B
OFF_GA = OFF_ZB + W_B
OFF_GB = OFF_GA + D_MODEL
D_IN = OFF_GB + D_MODEL

TQ = 256
ROWS_PER_TILE = TQ // GRID_W
KEY_TILES = 3
STEP_TILES = 1
KV_SLOTS = 2 * STEP_TILES + 1
Q_SLOTS = 2 * STEP_TILES
WIN_STACK = 2
NA_QK_ROWS = 256
WIN_QK_ROWS = 128
LOOKAHEAD = 2
VMEM_LIMIT = 56 * 1024 * 1024


def _const_spec(shape):
    return pl.BlockSpec(shape, lambda *_: (0,) * len(shape))


def _project_tile(x_ref, w_ref, bd_ref, gains_ref,
                  ka_ref, vat_ref, kb_ref, vbt_ref, qat_ref, za_ref, qbt_ref, zb_ref, ga_ref, gb_ref):
    x = x_ref[...]
    inv_rms = lax.rsqrt(jnp.mean(x * x, axis=-1, keepdims=True) + EPS)
    xb = x.astype(BF16)

    def proj(lo, width):
        return jnp.dot(xb, w_ref[:, lo:lo + width], preferred_element_type=F32) * inv_rms

    def head_norm(p, gain):
        width = p.shape[-1]
        sq = (p * p).astype(BF16)
        parts = []
        for lo in range(0, width, LANE_GROUP):
            w = min(LANE_GROUP, width - lo)
            parts.append(jnp.dot(sq[:, lo:lo + w], bd_ref[:w, :w], preferred_element_type=F32))
        ssq = parts[0] if len(parts) == 1 else jnp.concatenate(parts, axis=-1)
        return p * lax.rsqrt(ssq * (1.0 / HEAD_DIM) + EPS) * gain

    def head_norm_t(p):
        p_t = p.T
        heads = []
        for lo in range(0, p_t.shape[0], HEAD_DIM):
            head = p_t[lo:lo + HEAD_DIM]
            ms = jnp.sum(head * head, axis=0, keepdims=True) * (1.0 / HEAD_DIM)
            heads.append(head * lax.rsqrt(ms + EPS))
        return jnp.concatenate(heads, axis=0)

    def sigmoid(z):
        return 0.5 * jnp.tanh(0.5 * z) + 0.5

    def silu(z):
        return z * sigmoid(z)

    def spread_kv(kv):
        swapped = pltpu.roll(kv, HEAD_DIM, 1)
        low = lax.broadcasted_iota(jnp.int32, kv.shape, 1) < HEAD_DIM
        first = jnp.where(low, kv, swapped).astype(BF16)
        second = jnp.where(low, swapped, kv).astype(BF16)
        return jnp.concatenate([first, first, second, second], axis=-1)

    def store(ref, value, c=None):
        if c is None:
            ref[...] = value().astype(BF16)
        else:
            ref[:, c:c + W_A] = value().astype(BF16)

    def kv_b():
        kv = proj(OFF_KB, 2 * W_KV_B)
        kb_ref[...] = spread_kv(head_norm(kv[:, :W_KV_B], gains_ref[1:2, :W_KV_B]))
        vbt_ref[...] = kv[:, W_KV_B:].T.astype(BF16)

    early = [
        lambda: store(ka_ref, lambda: head_norm_t(proj(OFF_KA, W_A)).T * gains_ref[0:1, :]),
        lambda: store(vat_ref, lambda: proj(OFF_VA, W_A).T),
        kv_b,
    ]
    late = [
        lambda: store(qat_ref, lambda: head_norm_t(proj(OFF_QA, W_A))),
        lambda: store(qbt_ref, lambda: head_norm_t(proj(OFF_QB, W_B))),
        lambda: store(za_ref, lambda: silu(proj(OFF_ZA, W_A))),
        lambda: store(zb_ref, lambda: silu(proj(OFF_ZB, W_B))),
    ]
    for c in range(0, D_MODEL, W_A):
        late.append(lambda c=c: store(ga_ref, lambda: sigmoid(proj(OFF_GA + c, W_A)), c))
        late.append(lambda c=c: store(gb_ref, lambda: sigmoid(proj(OFF_GB + c, W_A)), c))
    return early, late


def _reduce_keys(x, combine, reduce):
    while x.shape[0] % (2 * SUBLANES) == 0:
        half = x.shape[0] // 2
        x = combine(x[:half], x[half:])
    return reduce(x, axis=0, keepdims=True)


def _attend_t(s_t, v_t, sink=None):
    m = _reduce_keys(s_t, jnp.maximum, jnp.max)
    if sink is not None:
        m = jnp.maximum(m, sink)
    p = jnp.exp2((s_t - m).astype(BF16))
    v_ones = jnp.concatenate([v_t, jnp.ones((ONES_ROWS, v_t.shape[1]), BF16)], axis=0)
    o = jnp.dot(v_ones, p, preferred_element_type=F32)
    l = o[HEAD_DIM:HEAD_DIM + 1]
    if sink is not None:
        l = l + jnp.exp2(sink - m)
    return o[:HEAD_DIM] * (1.0 / l)


def _head_rows(qt_ref, g, j, queries):
    lo = g * LANE_GROUP + j * HEAD_DIM
    head = qt_ref[lo:lo + HEAD_DIM, queries]
    above = jnp.zeros((j * HEAD_DIM, head.shape[1]), BF16)
    below = jnp.zeros(((HEADS_PER_GROUP - 1 - j) * HEAD_DIM, head.shape[1]), BF16)
    return jnp.concatenate([part for part in (above, head, below) if part.shape[0]], axis=0)


def _qk_t(keys, qt_masked, chunk):
    parts = [jnp.dot(keys[r:r + chunk], qt_masked, preferred_element_type=F32)
             for r in range(0, keys.shape[0], chunk)]
    return jnp.concatenate(parts, axis=0)


def _na_blocks(qt_ref, k_refs, vt_refs, bias_ref, case):
    blocks = []
    for g in range(W_A // LANE_GROUP):
        cols = slice(g * LANE_GROUP, (g + 1) * LANE_GROUP)
        for j in range(HEADS_PER_GROUP):
            h = g * HEADS_PER_GROUP + j

            def logits(g=g, cols=cols, j=j):
                k4 = jnp.concatenate([r[:, cols] for r in k_refs], axis=0)
                return _qk_t(k4, _head_rows(qt_ref, g, j, slice(None)), NA_QK_ROWS)

            def finish(s_t, h=h):
                feat = slice(h * HEAD_DIM, (h + 1) * HEAD_DIM)
                v_t = jnp.concatenate([r[feat, :] for r in vt_refs], axis=1)
                return _attend_t(s_t + bias_ref[case, h].astype(F32), v_t)

            blocks.append((logits, finish))
    return blocks


def _win_blocks(qt_ref, k_refs, vt_refs, bias_ref, sink_ref, t, nt):
    n_sub = TQ // BLOCK
    blocks = []
    for g in range(KVH_B):
        cols = slice(g * LANE_GROUP, (g + 1) * LANE_GROUP)
        for sub in range(n_sub):
            rows = slice(sub * BLOCK, (sub + 1) * BLOCK)
            key_lo = TQ + (sub - 1) * BLOCK
            if sub == 0:
                case = jnp.where(t == 0, 0, 1)
            elif sub == n_sub - 1:
                case = jnp.where(t == nt - 1, 2, 1)
            else:
                case = 1
            for pair in range(G_B // WIN_STACK):

                def logits(g=g, cols=cols, rows=rows, key_lo=key_lo, pair=pair):
                    kwin = jnp.concatenate([r[:, cols] for r in k_refs], axis=0)
                    heads = range(pair * WIN_STACK, (pair + 1) * WIN_STACK)
                    qt_pair = jnp.concatenate([_head_rows(qt_ref, g, j, rows) for j in heads], axis=1)
                    return _qk_t(kwin[key_lo:key_lo + 3 * BLOCK], qt_pair, WIN_QK_ROWS)

                def finish(s_t, g=g, key_lo=key_lo, pair=pair, case=case):
                    vwin_t = jnp.concatenate([r[g * HEAD_DIM:(g + 1) * HEAD_DIM, :] for r in vt_refs], axis=1)
                    v3_t = vwin_t[:, key_lo:key_lo + 3 * BLOCK]
                    return _attend_t(s_t + bias_ref[case, g, pair].astype(F32), v3_t,
                                     sink_ref[g * (G_B // WIN_STACK) + pair])

                blocks.append((logits, finish))
    return blocks


def _run_interleaved(blocks, fillers):
    outs, pending = [], []
    fillers = list(fillers)
    every = max(1, len(blocks) // max(1, len(fillers)))
    for i, (logits, finish) in enumerate(blocks):
        pending.append((logits(), finish))
        if len(pending) > LOOKAHEAD:
            s_t, fin = pending.pop(0)
            outs.append(fin(s_t))
        if fillers and i % every == every - 1:
            fillers.pop(0)()
    outs += [fin(s_t) for s_t, fin in pending]
    for filler in fillers:
        filler()
    return outs


def _assemble_win(outs):
    n_sub, n_pair = TQ // BLOCK, G_B // WIN_STACK
    feats = []
    for g in range(KVH_B):
        per_sub = []
        for sub in range(n_sub):
            pairs = outs[(g * n_sub + sub) * n_pair:(g * n_sub + sub + 1) * n_pair]
            heads = [o[:, i * BLOCK:(i + 1) * BLOCK] for o in pairs for i in range(WIN_STACK)]
            per_sub.append(jnp.concatenate(heads, axis=0))
        feats.append(jnp.concatenate(per_sub, axis=1))
    return jnp.concatenate(feats, axis=0)


def _layer_kernel(xp_ref, xr_ref, w_ref, bd_ref, gains_ref, bias_a_ref, bias_b_ref, sink_ref,
                  woa_ref, wob_ref, wout_ref, y_ref,
                  ka_ring, vat_ring, kb_ring, vbt_ring, qat_buf, za_buf, qbt_buf, zb_buf, ga_buf, gb_buf, *, nt):
    s = pl.program_id(0)
    kv_rings = (ka_ring, vat_ring, kb_ring, vbt_ring)
    q_bufs = (qat_buf, za_buf, qbt_buf, zb_buf, ga_buf, gb_buf)

    @pl.when(s == 0)
    def _():
        for buf in kv_rings + q_bufs:
            buf[...] = jnp.zeros(buf.shape, buf.dtype)

    kv_slot = lambda tile: lax.rem(tile + 2 * KV_SLOTS, KV_SLOTS)
    q_slot = lambda tile: lax.rem(tile + 2 * Q_SLOTS, Q_SLOTS)

    fillers = []
    for u in range(STEP_TILES):
        tile = s * STEP_TILES + u
        early, late = _project_tile(
            xp_ref.at[u * TQ:(u + 1) * TQ, :], w_ref, bd_ref, gains_ref,
            *[ring.at[kv_slot(tile)] for ring in kv_rings],
            qat_buf.at[q_slot(tile)], za_buf.at[q_slot(tile)], qbt_buf.at[q_slot(tile)], zb_buf.at[q_slot(tile)],
            ga_buf.at[q_slot(tile)], gb_buf.at[q_slot(tile)])
        if u == 0:
            for chunk in early:
                chunk()
            fillers += late
        else:
            fillers += early + late

    blocks, attended = [], []
    for u in range(STEP_TILES):
        tile = (s - 1) * STEP_TILES + u
        t = lax.rem(jnp.maximum(tile, 0), nt)
        case = jnp.where(t == 0, 0, jnp.where(t == nt - 1, 2, 1))
        window = [kv_slot(tile + d) for d in (-1, 0, 1)]
        na = _na_blocks(qat_buf.at[q_slot(tile)], [ka_ring.at[i] for i in window],
                        [vat_ring.at[i] for i in window], bias_a_ref, case)
        win = _win_blocks(qbt_buf.at[q_slot(tile)], [kb_ring.at[i] for i in window],
                          [vbt_ring.at[i] for i in window], bias_b_ref, sink_ref, t, nt)
        attended.append((q_slot(tile), len(blocks), len(na), len(win)))
        blocks += na + win
    outs = _run_interleaved(blocks, fillers)

    oa, ob, ga, gb = [], [], [], []
    for slot, lo, n_na, n_win in attended:
        oa.append((jnp.concatenate(outs[lo:lo + n_na], axis=0).T * za_buf[slot].astype(F32)).astype(BF16))
        ob.append((_assemble_win(outs[lo + n_na:lo + n_na + n_win]).T * zb_buf[slot].astype(F32)).astype(BF16))
        ga.append(ga_buf[slot].astype(F32))
        gb.append(gb_buf[slot].astype(F32))
    rows = lambda parts: jnp.concatenate(parts, axis=0)
    a = jnp.dot(rows(oa), woa_ref[...], preferred_element_type=F32)
    b = jnp.dot(rows(ob), wob_ref[...], preferred_element_type=F32)
    merged = rows(ga) * a + rows(gb) * b
    y_ref[...] = xr_ref[...] + jnp.dot(merged.astype(BF16), wout_ref[...], preferred_element_type=F32)


def _layer_call(x2d, w_bf, bd, gains, bias_a, bias_b, sink, woa, wob, wout, seq):
    n_steps = x2d.shape[0] // (TQ * STEP_TILES)
    nt = seq // TQ
    assert seq % TQ == 0 and nt >= 3 and x2d.shape[0] % (TQ * STEP_TILES) == 0
    proj_step = lambda s: (jnp.minimum(s, n_steps - 1), 0)
    attn_step = lambda s: (jnp.maximum(s - 1, 0), 0)
    consts = (w_bf, bd, gains, bias_a, bias_b, sink, woa, wob, wout)
    ring = lambda slots, rows, cols: pltpu.VMEM((slots, rows, cols), BF16)
    block = (TQ * STEP_TILES, D_MODEL)
    return pl.pallas_call(
        functools.partial(_layer_kernel, nt=nt),
        grid=(n_steps + 1,),
        in_specs=[pl.BlockSpec(block, proj_step), pl.BlockSpec(block, attn_step)]
                 + [_const_spec(c.shape) for c in consts],
        out_specs=pl.BlockSpec(block, attn_step),
        out_shape=jax.ShapeDtypeStruct(x2d.shape, F32),
        scratch_shapes=[ring(KV_SLOTS, TQ, W_A), ring(KV_SLOTS, W_A, TQ), ring(KV_SLOTS, TQ, W_B),
                        ring(KV_SLOTS, W_KV_B, TQ), ring(Q_SLOTS, W_A, TQ), ring(Q_SLOTS, TQ, W_A),
                        ring(Q_SLOTS, W_B, TQ), ring(Q_SLOTS, TQ, W_B), ring(Q_SLOTS, TQ, D_MODEL),
                        ring(Q_SLOTS, TQ, D_MODEL)],
        compiler_params=pltpu.CompilerParams(dimension_semantics=("arbitrary",),
                                             vmem_limit_bytes=VMEM_LIMIT),
        name="encoder_layer",
    )(x2d, x2d, *consts)


def _toeplitz(v, m, n):
    w = v.shape[-1]
    assert m <= w and n <= w
    doubled = jnp.concatenate([v, v], axis=-1)
    return jnp.stack([doubled[..., w - q:w - q + n] for q in range(m)], axis=-2)


def _na_bias(rpb, rows):
    kr = min(NA_ROWS, rows)
    assert kr == NA_ROWS and rows >= 3 * ROWS_PER_TILE
    n_key_rows = KEY_TILES * ROWS_PER_TILE
    rpb = rpb.astype(F32) * LOG2E
    ring_pad = jnp.zeros(rpb.shape[:-1] + (2 * GRID_W - (2 * NA_COLS - 1),), F32)
    ring = jnp.concatenate([rpb[..., NA_COLS - 1:], ring_pad, rpb[..., :NA_COLS - 1]], axis=-1)
    col = _toeplitz(ring, GRID_W, GRID_W).astype(BIAS_DTYPE)
    first = NA_ROWS - 1 - ROWS_PER_TILE
    per_row = [col[:, first - i:first - i + n_key_rows] for i in range(ROWS_PER_TILE)]
    vals_t = jnp.transpose(jnp.stack(per_row, axis=1), (0, 2, 4, 1, 3)).reshape(H_A, KEY_TILES * TQ, TQ)
    i = np.arange(ROWS_PER_TILE)[:, None, None, None]
    qc = np.arange(GRID_W)[None, :, None, None]
    j = np.arange(n_key_rows)[None, None, :, None]
    kc = np.arange(GRID_W)[None, None, None, :]
    cs = np.clip(qc - NA_COLS // 2, 0, GRID_W - NA_COLS)
    col_ok = (kc >= cs) & (kc < cs + NA_COLS)
    shape = (ROWS_PER_TILE, GRID_W, n_key_rows, GRID_W)
    row_ok = [
        (j >= ROWS_PER_TILE) & (j < ROWS_PER_TILE + kr),
        (j - i >= 0) & (j - i < kr),
        (j >= 0) & (j < kr),
    ]
    out = []
    for ok in row_ok:
        valid_t = np.broadcast_to(ok & col_ok, shape).transpose(2, 3, 0, 1).reshape(KEY_TILES * TQ, TQ)
        out.append(jnp.where(valid_t[None], vals_t, jnp.asarray(NEG, BIAS_DTYPE)))
    return jnp.stack(out)


def _t5_bucket(rel):
    half = T5_BUCKETS // 2
    max_exact = half // 2
    n = jnp.abs(rel)
    large = max_exact + (jnp.log(jnp.maximum(n, 1).astype(jnp.float32) / max_exact)
                         / math.log(T5_MAX_DIST / max_exact) * (half - max_exact)).astype(jnp.int32)
    large = jnp.minimum(large, half - 1)
    return jnp.where(rel > 0, half, 0) + jnp.where(n < max_exact, n, large)


def _win_bias(t5_table):
    ring = 4 * BLOCK
    u = jnp.arange(ring)
    rel = jnp.where(u < ring // 2, u, u - ring)
    onehot = (_t5_bucket(rel)[:, None] == jnp.arange(T5_BUCKETS)[None, :]).astype(F32)
    by_rel = jnp.dot(onehot, t5_table.astype(F32) * LOG2E, precision=lax.Precision.HIGHEST)
    by_rel = jnp.where((jnp.abs(rel) <= WINDOW)[:, None], by_rel, NEG).T
    bias = _toeplitz(jnp.roll(by_rel, BLOCK, axis=-1), BLOCK, 3 * BLOCK)
    blk = np.arange(3 * BLOCK) // BLOCK
    cases = [jnp.where((blk != 0)[None, None], bias, NEG), bias, jnp.where((blk != 2)[None, None], bias, NEG)]
    shape = (KVH_B, G_B // WIN_STACK, WIN_STACK * BLOCK, 3 * BLOCK)
    return jnp.stack([jnp.swapaxes(c.reshape(shape), -1, -2) for c in cases]).astype(BIAS_DTYPE)


def _layer_constants(qn_a, kn_a, qn_b, kn_b):
    q_scale = LOG2E / math.sqrt(HEAD_DIM)
    tile = lambda gvec: jnp.tile(gvec.astype(F32), W_A // HEAD_DIM)
    gains = jnp.stack([tile(qn_a) * tile(kn_a) * q_scale, tile(qn_b) * tile(kn_b) * q_scale])
    lane = np.arange(LANE_GROUP)
    bd = jnp.asarray(lane[:, None] // HEAD_DIM == lane[None, :] // HEAD_DIM, BF16)
    return gains, bd


def _encoder_layer(x, norm_g, w_in, qn_a, kn_a, rpb_a, qn_b, kn_b, sink_b, w_o_a, w_o_b, w_out, t5_table):
    batch, seq, d = x.shape
    assert d == D_MODEL and seq % GRID_W == 0
    x2d = x.reshape(batch * seq, d)
    gains, bd = _layer_constants(qn_a, kn_a, qn_b, kn_b)
    w_bf = (norm_g.astype(F32)[:, None] * w_in).astype(BF16)
    sink = jnp.repeat(sink_b.astype(F32) * LOG2E, BLOCK).reshape(H_B // WIN_STACK, 1, WIN_STACK * BLOCK)
    y = _layer_call(x2d, w_bf, bd, gains, _na_bias(rpb_a, seq // GRID_W), _win_bias(t5_table), sink,
                    w_o_a.astype(BF16), w_o_b.astype(BF16), w_out.astype(BF16), seq)
    return y.reshape(batch, seq, d)


def kernel(x_prompt, x_sample, norm_g, w_in, qn_a, kn_a, rpb_a, qn_b, kn_b, sink_b, w_o_a, w_o_b, w_out, t5_table):
    y_prompt, y_sample = x_prompt, x_sample
    for l in range(norm_g.shape[0]):
        params = (norm_g[l], w_in[l], qn_a[l], kn_a[l], rpb_a[l], qn_b[l], kn_b[l], sink_b[l],
                  w_o_a[l], w_o_b[l], w_out[l], t5_table)
        y_prompt = _encoder_layer(y_prompt, *params)
        y_sample = _encoder_layer(y_sample, *params)
    return (y_prompt, y_sample)
```

```python
import functools
import math

import jax
import jax.numpy as jnp
import numpy as np
from jax import lax
from jax.experimental import pallas as pl
from jax.experimental.pallas import tpu as pltpu

F32 = jnp.float32
BF16 = jnp.bfloat16
BIAS_DTYPE = BF16

D_MODEL = 1024
HEAD_DIM = 64
H_A = 8
H_B = 8
KVH_B = 2
G_B = H_B // KVH_B
W_A = H_A * HEAD_DIM
W_B = H_B * HEAD_DIM
W_KV_B = KVH_B * HEAD_DIM
GRID_W = 64
NA_ROWS = 8
NA_COLS = 16
WINDOW = 128
BLOCK = 128
T5_BUCKETS = 32
T5_MAX_DIST = 128
EPS = 1e-6
NEG = -1e30
LOG2E = math.log2(math.e)

SUBLANES = 8
ONES_ROWS = 16
LANE_GROUP = 256
HEADS_PER_GROUP = LANE_GROUP // HEAD_DIM
OFF_QA, OFF_KA, OFF_VA, OFF_ZA = 0, W_A, 2 * W_A, 3 * W_A
OFF_QB = 4 * W_A
OFF_KB = OFF_QB + W_B
OFF_VB = OFF_KB + W_KV_B
OFF_ZB = OFF_VB + W_KV_B
OFF_GA = OFF_ZB + W_B
OFF_GB = OFF_GA + D_MODEL
D_IN = OFF_GB + D_MODEL

TQ = 256
ROWS_PER_TILE = TQ // GRID_W
KEY_TILES = 3
STEP_TILES = 1
KV_SLOTS = 2 * STEP_TILES + 1
Q_SLOTS = 2 * STEP_TILES
WIN_STACK = 2
NA_QK_ROWS = 256
WIN_QK_ROWS = 128
X_BUFFERS = 2
LOOKAHEAD = 2
VMEM_LIMIT = 56 * 1024 * 1024


def _const_spec(shape):
    return pl.BlockSpec(shape, lambda *_: (0,) * len(shape))


def _project_tile(x_ref, w_ref, bd_ref, gains_ref,
                  ka_ref, vat_ref, kb_ref, vbt_ref, qat_ref, za_ref, qbt_ref, zb_ref, ga_ref, gb_ref):
    x = x_ref[...]
    inv_rms = lax.rsqrt(jnp.mean(x * x, axis=-1, keepdims=True) + EPS)
    xb = x.astype(BF16)

    def proj(lo, width):
        return jnp.dot(xb, w_ref[:, lo:lo + width], preferred_element_type=F32) * inv_rms

    def head_norm(p, gain):
        width = p.shape[-1]
        sq = (p * p).astype(BF16)
        parts = []
        for lo in range(0, width, LANE_GROUP):
            w = min(LANE_GROUP, width - lo)
            parts.append(jnp.dot(sq[:, lo:lo + w], bd_ref[:w, :w], preferred_element_type=F32))
        ssq = parts[0] if len(parts) == 1 else jnp.concatenate(parts, axis=-1)
        return p * lax.rsqrt(ssq * (1.0 / HEAD_DIM) + EPS) * gain

    def head_norm_t(p):
        p_t = p.T
        heads = []
        for lo in range(0, p_t.shape[0], HEAD_DIM):
            head = p_t[lo:lo + HEAD_DIM]
            ms = jnp.sum(head * head, axis=0, keepdims=True) * (1.0 / HEAD_DIM)
            heads.append(head * lax.rsqrt(ms + EPS))
        return jnp.concatenate(heads, axis=0)

    def sigmoid(z):
        return 0.5 * jnp.tanh(0.5 * z) + 0.5

    def silu(z):
        return z * sigmoid(z)

    def spread_kv(kv):
        swapped = pltpu.roll(kv, HEAD_DIM, 1)
        low = lax.broadcasted_iota(jnp.int32, kv.shape, 1) < HEAD_DIM
        first = jnp.where(low, kv, swapped).astype(BF16)
        second = jnp.where(low, swapped, kv).astype(BF16)
        return jnp.concatenate([first, first, second, second], axis=-1)

    def store(ref, value, c=None):
        if c is None:
            ref[...] = value().astype(BF16)
        else:
            ref[:, c:c + W_A] = value().astype(BF16)

    def kv_b():
        kv = proj(OFF_KB, 2 * W_KV_B)
        kb_ref[...] = spread_kv(head_norm(kv[:, :W_KV_B], gains_ref[1:2, :W_KV_B]))
        vbt_ref[...] = kv[:, W_KV_B:].T.astype(BF16)

    early = [
        lambda: store(ka_ref, lambda: head_norm_t(proj(OFF_KA, W_A)).T * gains_ref[0:1, :]),
        lambda: store(vat_ref, lambda: proj(OFF_VA, W_A).T),
        kv_b,
    ]
    late = [
        lambda: store(qat_ref, lambda: head_norm_t(proj(OFF_QA, W_A))),
        lambda: store(qbt_ref, lambda: head_norm_t(proj(OFF_QB, W_B))),
        lambda: store(za_ref, lambda: silu(proj(OFF_ZA, W_A))),
        lambda: store(zb_ref, lambda: silu(proj(OFF_ZB, W_B))),
    ]
    for c in range(0, D_MODEL, W_A):
        late.append(lambda c=c: store(ga_ref, lambda: sigmoid(proj(OFF_GA + c, W_A)), c))
        late.append(lambda c=c: store(gb_ref, lambda: sigmoid(proj(OFF_GB + c, W_A)), c))
    return early, late


def _reduce_keys(x, combine, reduce):
    while x.shape[0] % (2 * SUBLANES) == 0:
        half = x.shape[0] // 2
        x = combine(x[:half], x[half:])
    return reduce(x, axis=0, keepdims=True)


def _attend_t(s_t, v_t, sink=None):
    m = _reduce_keys(s_t, jnp.maximum, jnp.max)
    if sink is not None:
        m = jnp.maximum(m, sink)
    p = jnp.exp2((s_t - m).astype(BF16))
    v_ones = jnp.concatenate([v_t, jnp.ones((ONES_ROWS, v_t.shape[1]), BF16)], axis=0)
    o = jnp.dot(v_ones, p, preferred_element_type=F32)
    l = o[HEAD_DIM:HEAD_DIM + 1]
    if sink is not None:
        l = l + jnp.exp2(sink - m)
    return o[:HEAD_DIM] * (1.0 / l)


def _head_rows(qt_ref, g, j, queries):
    lo = g * LANE_GROUP + j * HEAD_DIM
    head = qt_ref[lo:lo + HEAD_DIM, queries]
    above = jnp.zeros((j * HEAD_DIM, head.shape[1]), BF16)
    below = jnp.zeros(((HEADS_PER_GROUP - 1 - j) * HEAD_DIM, head.shape[1]), BF16)
    return jnp.concatenate([part for part in (above, head, below) if part.shape[0]], axis=0)


def _qk_t(keys, qt_masked, chunk):
    parts = [jnp.dot(keys[r:r + chunk], qt_masked, preferred_element_type=F32)
             for r in range(0, keys.shape[0], chunk)]
    return jnp.concatenate(parts, axis=0)


def _na_blocks(qt_ref, k_refs, vt_refs, bias_ref, case):
    blocks = []
    for g in range(W_A // LANE_GROUP):
        cols = slice(g * LANE_GROUP, (g + 1) * LANE_GROUP)
        for j in range(HEADS_PER_GROUP):
            h = g * HEADS_PER_GROUP + j

            def logits(g=g, cols=cols, j=j):
                k4 = jnp.concatenate([r[:, cols] for r in k_refs], axis=0)
                return _qk_t(k4, _head_rows(qt_ref, g, j, slice(None)), NA_QK_ROWS)

            def finish(s_t, h=h):
                feat = slice(h * HEAD_DIM, (h + 1) * HEAD_DIM)
                v_t = jnp.concatenate([r[feat, :] for r in vt_refs], axis=1)
                return _attend_t(s_t + bias_ref[case, h].astype(F32), v_t)

            blocks.append((logits, finish))
    return blocks


def _win_blocks(qt_ref, k_refs, vt_refs, bias_ref, sink_ref, t, nt):
    n_sub = TQ // BLOCK
    blocks = []
    for g in range(KVH_B):
        cols = slice(g * LANE_GROUP, (g + 1) * LANE_GROUP)
        for sub in range(n_sub):
            rows = slice(sub * BLOCK, (sub + 1) * BLOCK)
            key_lo = TQ + (sub - 1) * BLOCK
            if sub == 0:
                case = jnp.where(t == 0, 0, 1)
            elif sub == n_sub - 1:
                case = jnp.where(t == nt - 1, 2, 1)
            else:
                case = 1
            for pair in range(G_B // WIN_STACK):

                def logits(g=g, cols=cols, rows=rows, key_lo=key_lo, pair=pair):
                    kwin = jnp.concatenate([r[:, cols] for r in k_refs], axis=0)
                    heads = range(pair * WIN_STACK, (pair + 1) * WIN_STACK)
                    qt_pair = jnp.concatenate([_head_rows(qt_ref, g, j, rows) for j in heads], axis=1)
                    return _qk_t(kwin[key_lo:key_lo + 3 * BLOCK], qt_pair, WIN_QK_ROWS)

                def finish(s_t, g=g, key_lo=key_lo, pair=pair, case=case):
                    vwin_t = jnp.concatenate([r[g * HEAD_DIM:(g + 1) * HEAD_DIM, :] for r in vt_refs], axis=1)
                    v3_t = vwin_t[:, key_lo:key_lo + 3 * BLOCK]
                    return _attend_t(s_t + bias_ref[case, g, pair].astype(F32), v3_t,
                                     sink_ref[g * (G_B // WIN_STACK) + pair])

                blocks.append((logits, finish))
    return blocks


def _run_interleaved(blocks, fillers):
    outs, pending = [], []
    fillers = list(fillers)
    every = max(1, len(blocks) // max(1, len(fillers)))
    for i, (logits, finish) in enumerate(blocks):
        pending.append((logits(), finish))
        if len(pending) > LOOKAHEAD:
            s_t, fin = pending.pop(0)
            outs.append(fin(s_t))
        if fillers and i % every == every - 1:
            fillers.pop(0)()
    outs += [fin(s_t) for s_t, fin in pending]
    for filler in fillers:
        filler()
    return outs


def _assemble_win(outs):
    n_sub, n_pair = TQ // BLOCK, G_B // WIN_STACK
    feats = []
    for g in range(KVH_B):
        per_sub = []
        for sub in range(n_sub):
            pairs = outs[(g * n_sub + sub) * n_pair:(g * n_sub + sub + 1) * n_pair]
            heads = [o[:, i * BLOCK:(i + 1) * BLOCK] for o in pairs for i in range(WIN_STACK)]
            per_sub.append(jnp.concatenate(heads, axis=0))
        feats.append(jnp.concatenate(per_sub, axis=1))
    return jnp.concatenate(feats, axis=0)


def _layer_kernel(xp_ref, w_ref, bd_ref, gains_ref, bias_a_ref, bias_b_ref, sink_ref,
                  woa_ref, wob_ref, wout_ref, y_ref,
                  ka_ring, vat_ring, kb_ring, vbt_ring, qat_buf, za_buf, qbt_buf, zb_buf, ga_buf, gb_buf,
                  x_buf, *, nt):
    s = pl.program_id(0)
    kv_rings = (ka_ring, vat_ring, kb_ring, vbt_ring)
    q_bufs = (qat_buf, za_buf, qbt_buf, zb_buf, ga_buf, gb_buf, x_buf)

    @pl.when(s == 0)
    def _():
        for buf in kv_rings + q_bufs:
            buf[...] = jnp.zeros(buf.shape, buf.dtype)

    kv_slot = lambda tile: lax.rem(tile + 2 * KV_SLOTS, KV_SLOTS)
    q_slot = lambda tile: lax.rem(tile + 2 * Q_SLOTS, Q_SLOTS)

    fillers = []
    for u in range(STEP_TILES):
        tile = s * STEP_TILES + u
        early, late = _project_tile(
            xp_ref.at[u * TQ:(u + 1) * TQ, :], w_ref, bd_ref, gains_ref,
            *[ring.at[kv_slot(tile)] for ring in kv_rings],
            qat_buf.at[q_slot(tile)], za_buf.at[q_slot(tile)], qbt_buf.at[q_slot(tile)], zb_buf.at[q_slot(tile)],
            ga_buf.at[q_slot(tile)], gb_buf.at[q_slot(tile)])
        x_buf[q_slot(tile)] = xp_ref[u * TQ:(u + 1) * TQ, :]
        if u == 0:
            for chunk in early:
                chunk()
            fillers += late
        else:
            fillers += early + late

    blocks, attended = [], []
    for u in range(STEP_TILES):
        tile = (s - 1) * STEP_TILES + u
        t = lax.rem(jnp.maximum(tile, 0), nt)
        case = jnp.where(t == 0, 0, jnp.where(t == nt - 1, 2, 1))
        window = [kv_slot(tile + d) for d in (-1, 0, 1)]
        na = _na_blocks(qat_buf.at[q_slot(tile)], [ka_ring.at[i] for i in window],
                        [vat_ring.at[i] for i in window], bias_a_ref, case)
        win = _win_blocks(qbt_buf.at[q_slot(tile)], [kb_ring.at[i] for i in window],
                          [vbt_ring.at[i] for i in window], bias_b_ref, sink_ref, t, nt)
        attended.append((q_slot(tile), len(blocks), len(na), len(win)))
        blocks += na + win
    outs = _run_interleaved(blocks, fillers)

    oa, ob, ga, gb, x_res = [], [], [], [], []
    for slot, lo, n_na, n_win in attended:
        oa.append((jnp.concatenate(outs[lo:lo + n_na], axis=0).T * za_buf[slot].astype(F32)).astype(BF16))
        ob.append((_assemble_win(outs[lo + n_na:lo + n_na + n_win]).T * zb_buf[slot].astype(F32)).astype(BF16))
        ga.append(ga_buf[slot].astype(F32))
        gb.append(gb_buf[slot].astype(F32))
        x_res.append(x_buf[slot])
    rows = lambda parts: jnp.concatenate(parts, axis=0)
    a = jnp.dot(rows(oa), woa_ref[...], preferred_element_type=F32)
    b = jnp.dot(rows(ob), wob_ref[...], preferred_element_type=F32)
    merged = rows(ga) * a + rows(gb) * b
    y_ref[...] = rows(x_res) + jnp.dot(merged.astype(BF16), wout_ref[...], preferred_element_type=F32)


def _layer_call(x2d, w_bf, bd, gains, bias_a, bias_b, sink, woa, wob, wout, seq):
    n_steps = x2d.shape[0] // (TQ * STEP_TILES)
    nt = seq // TQ
    assert seq % TQ == 0 and nt >= 3 and x2d.shape[0] % (TQ * STEP_TILES) == 0
    proj_step = lambda s: (jnp.minimum(s, n_steps - 1), 0)
    attn_step = lambda s: (jnp.maximum(s - 1, 0), 0)
    consts = (w_bf, bd, gains, bias_a, bias_b, sink, woa, wob, wout)
    ring = lambda slots, rows, cols: pltpu.VMEM((slots, rows, cols), BF16)
    block = (TQ * STEP_TILES, D_MODEL)
    return pl.pallas_call(
        functools.partial(_layer_kernel, nt=nt),
        grid=(n_steps + 1,),
        in_specs=[pl.BlockSpec(block, proj_step, pipeline_mode=pl.Buffered(X_BUFFERS))]
                 + [_const_spec(c.shape) for c in consts],
        out_specs=pl.BlockSpec(block, attn_step),
        out_shape=jax.ShapeDtypeStruct(x2d.shape, F32),
        scratch_shapes=[ring(KV_SLOTS, TQ, W_A), ring(KV_SLOTS, W_A, TQ), ring(KV_SLOTS, TQ, W_B),
                        ring(KV_SLOTS, W_KV_B, TQ), ring(Q_SLOTS, W_A, TQ), ring(Q_SLOTS, TQ, W_A),
                        ring(Q_SLOTS, W_B, TQ), ring(Q_SLOTS, TQ, W_B), ring(Q_SLOTS, TQ, D_MODEL),
                        ring(Q_SLOTS, TQ, D_MODEL), pltpu.VMEM((Q_SLOTS, TQ, D_MODEL), F32)],
        compiler_params=pltpu.CompilerParams(dimension_semantics=("arbitrary",),
                                             vmem_limit_bytes=VMEM_LIMIT),
        name="encoder_layer",
    )(x2d, *consts)


def _toeplitz(v, m, n):
    w = v.shape[-1]
    assert m <= w and n <= w
    doubled = jnp.concatenate([v, v], axis=-1)
    return jnp.stack([doubled[..., w - q:w - q + n] for q in range(m)], axis=-2)


def _na_bias(rpb, rows):
    kr = min(NA_ROWS, rows)
    assert kr == NA_ROWS and rows >= 3 * ROWS_PER_TILE
    n_key_rows = KEY_TILES * ROWS_PER_TILE
    rpb = rpb.astype(F32) * LOG2E
    ring_pad = jnp.zeros(rpb.shape[:-1] + (2 * GRID_W - (2 * NA_COLS - 1),), F32)
    ring = jnp.concatenate([rpb[..., NA_COLS - 1:], ring_pad, rpb[..., :NA_COLS - 1]], axis=-1)
    col = _toeplitz(ring, GRID_W, GRID_W).astype(BIAS_DTYPE)
    first = NA_ROWS - 1 - ROWS_PER_TILE
    per_row = [col[:, first - i:first - i + n_key_rows] for i in range(ROWS_PER_TILE)]
    vals_t = jnp.transpose(jnp.stack(per_row, axis=1), (0, 2, 4, 1, 3)).reshape(H_A, KEY_TILES * TQ, TQ)
    i = np.arange(ROWS_PER_TILE)[:, None, None, None]
    qc = np.arange(GRID_W)[None, :, None, None]
    j = np.arange(n_key_rows)[None, None, :, None]
    kc = np.arange(GRID_W)[None, None, None, :]
    cs = np.clip(qc - NA_COLS // 2, 0, GRID_W - NA_COLS)
    col_ok = (kc >= cs) & (kc < cs + NA_COLS)
    shape = (ROWS_PER_TILE, GRID_W, n_key_rows, GRID_W)
    row_ok = [
        (j >= ROWS_PER_TILE) & (j < ROWS_PER_TILE + kr),
        (j - i >= 0) & (j - i < kr),
        (j >= 0) & (j < kr),
    ]
    out = []
    for ok in row_ok:
        valid_t = np.broadcast_to(ok & col_ok, shape).transpose(2, 3, 0, 1).reshape(KEY_TILES * TQ, TQ)
        out.append(jnp.where(valid_t[None], vals_t, jnp.asarray(NEG, BIAS_DTYPE)))
    return jnp.stack(out)


def _t5_bucket(rel):
    half = T5_BUCKETS // 2
    max_exact = half // 2
    n = jnp.abs(rel)
    large = max_exact + (jnp.log(jnp.maximum(n, 1).astype(jnp.float32) / max_exact)
                         / math.log(T5_MAX_DIST / max_exact) * (half - max_exact)).astype(jnp.int32)
    large = jnp.minimum(large, half - 1)
    return jnp.where(rel > 0, half, 0) + jnp.where(n < max_exact, n, large)


def _win_bias(t5_table):
    ring = 4 * BLOCK
    u = jnp.arange(ring)
    rel = jnp.where(u < ring // 2, u, u - ring)
    onehot = (_t5_bucket(rel)[:, None] == jnp.arange(T5_BUCKETS)[None, :]).astype(F32)
    by_rel = jnp.dot(onehot, t5_table.astype(F32) * LOG2E, precision=lax.Precision.HIGHEST)
    by_rel = jnp.where((jnp.abs(rel) <= WINDOW)[:, None], by_rel, NEG).T
    bias = _toeplitz(jnp.roll(by_rel, BLOCK, axis=-1), BLOCK, 3 * BLOCK)
    blk = np.arange(3 * BLOCK) // BLOCK
    cases = [jnp.where((blk != 0)[None, None], bias, NEG), bias, jnp.where((blk != 2)[None, None], bias, NEG)]
    shape = (KVH_B, G_B // WIN_STACK, WIN_STACK * BLOCK, 3 * BLOCK)
    return jnp.stack([jnp.swapaxes(c.reshape(shape), -1, -2) for c in cases]).astype(BIAS_DTYPE)


def _layer_constants(qn_a, kn_a, qn_b, kn_b):
    q_scale = LOG2E / math.sqrt(HEAD_DIM)
    tile = lambda gvec: jnp.tile(gvec.astype(F32), W_A // HEAD_DIM)
    gains = jnp.stack([tile(qn_a) * tile(kn_a) * q_scale, tile(qn_b) * tile(kn_b) * q_scale])
    lane = np.arange(LANE_GROUP)
    bd = jnp.asarray(lane[:, None] // HEAD_DIM == lane[None, :] // HEAD_DIM, BF16)
    return gains, bd


def _encoder_layer(x, norm_g, w_in, qn_a, kn_a, rpb_a, qn_b, kn_b, sink_b, w_o_a, w_o_b, w_out, t5_table):
    batch, seq, d = x.shape
    assert d == D_MODEL and seq % GRID_W == 0
    x2d = x.reshape(batch * seq, d)
    gains, bd = _layer_constants(qn_a, kn_a, qn_b, kn_b)
    w_bf = (norm_g.astype(F32)[:, None] * w_in).astype(BF16)
    sink = jnp.repeat(sink_b.astype(F32) * LOG2E, BLOCK).reshape(H_B // WIN_STACK, 1, WIN_STACK * BLOCK)
    y = _layer_call(x2d, w_bf, bd, gains, _na_bias(rpb_a, seq // GRID_W), _win_bias(t5_table), sink,
                    w_o_a.astype(BF16), w_o_b.astype(BF16), w_out.astype(BF16), seq)
    return y.reshape(batch, seq, d)


def kernel(x_prompt, x_sample, norm_g, w_in, qn_a, kn_a, rpb_a, qn_b, kn_b, sink_b, w_o_a, w_o_b, w_out, t5_table):
    y_prompt, y_sample = x_prompt, x_sample
    for l in range(norm_g.shape[0]):
        params = (norm_g[l], w_in[l], qn_a[l], kn_a[l], rpb_a[l], qn_b[l], kn_b[l], sink_b[l],
                  w_o_a[l], w_o_b[l], w_out[l], t5_table)
        y_prompt = _encoder_layer(y_prompt, *params)
        y_sample = _encoder_layer(y_sample, *params)
    return (y_prompt, y_sample)
```

```python
import functools
import math

import jax
import jax.numpy as jnp
import numpy as np
from jax import lax
from jax.experimental import pallas as pl
from jax.experimental.pallas import tpu as pltpu

F32 = jnp.float32
BF16 = jnp.bfloat16
BIAS_DTYPE = BF16

D_MODEL = 1024
HEAD_DIM = 64
H_A = 8
H_B = 8
KVH_B = 2
G_B = H_B // KVH_B
W_A = H_A * HEAD_DIM
W_B = H_B * HEAD_DIM
W_KV_B = KVH_B * HEAD_DIM
GRID_W = 64
NA_ROWS = 8
NA_COLS = 16
WINDOW = 128
BLOCK = 128
T5_BUCKETS = 32
T5_MAX_DIST = 128
EPS = 1e-6
NEG = -1e30
LOG2E = math.log2(math.e)

SUBLANES = 8
ONES_ROWS = 16
LANE_GROUP = 256
HEADS_PER_GROUP = LANE_GROUP // HEAD_DIM
OFF_QA, OFF_KA, OFF_VA, OFF_ZA = 0, W_A, 2 * W_A, 3 * W_A
OFF_QB = 4 * W_A
OFF_KB = OFF_QB + W_B
OFF_VB = OFF_KB + W_KV_B
OFF_ZB = OFF_VB + W_KV_B
OFF_GA = OFF_ZB + W_B
OFF_GB = OFF_GA + D_MODEL
D_IN = OFF_GB + D_MODEL
WT_QA, WT_VA, WT_QB = 0, W_A, 2 * W_A

TQ = 256
ROWS_PER_TILE = TQ // GRID_W
KEY_TILES = 3
STEP_TILES = 1
KV_SLOTS = 2 * STEP_TILES + 1
Q_SLOTS = 2 * STEP_TILES
WIN_STACK = 2
NA_QK_ROWS = 256
WIN_QK_ROWS = 128
LOOKAHEAD = 2
VMEM_LIMIT = 56 * 1024 * 1024


def _const_spec(shape):
    return pl.BlockSpec(shape, lambda *_: (0,) * len(shape))


def _project_tile(x_ref, w_ref, wt_ref, bd_ref, gains_ref,
                  ka_ref, vat_ref, kb_ref, vbt_ref, qat_ref, za_ref, qbt_ref, zb_ref, ga_ref, gb_ref):
    x = x_ref[...]
    inv_rms = lax.rsqrt(jnp.mean(x * x, axis=-1, keepdims=True) + EPS)
    inv_rms_t = jnp.broadcast_to(inv_rms, (x.shape[0], 128)).T[0:1]
    xb = x.astype(BF16)

    def proj(lo, width):
        return jnp.dot(xb, w_ref[:, lo:lo + width], preferred_element_type=F32) * inv_rms

    def proj_t(lo, width):
        return lax.dot_general(wt_ref[lo:lo + width, :], xb, (((1,), (1,)), ((), ())),
                               preferred_element_type=F32) * inv_rms_t

    def head_norm(p, gain):
        width = p.shape[-1]
        sq = (p * p).astype(BF16)
        parts = []
        for lo in range(0, width, LANE_GROUP):
            w = min(LANE_GROUP, width - lo)
            parts.append(jnp.dot(sq[:, lo:lo + w], bd_ref[:w, :w], preferred_element_type=F32))
        ssq = parts[0] if len(parts) == 1 else jnp.concatenate(parts, axis=-1)
        return p * lax.rsqrt(ssq * (1.0 / HEAD_DIM) + EPS) * gain

    def head_norm_t(p_t):
        heads = []
        for lo in range(0, p_t.shape[0], HEAD_DIM):
            head = p_t[lo:lo + HEAD_DIM]
            ms = jnp.sum(head * head, axis=0, keepdims=True) * (1.0 / HEAD_DIM)
            heads.append(head * lax.rsqrt(ms + EPS))
        return jnp.concatenate(heads, axis=0)

    def sigmoid(z):
        return 0.5 * jnp.tanh(0.5 * z) + 0.5

    def silu(z):
        return z * sigmoid(z)

    def spread_kv(kv):
        swapped = pltpu.roll(kv, HEAD_DIM, 1)
        low = lax.broadcasted_iota(jnp.int32, kv.shape, 1) < HEAD_DIM
        first = jnp.where(low, kv, swapped).astype(BF16)
        second = jnp.where(low, swapped, kv).astype(BF16)
        return jnp.concatenate([first, first, second, second], axis=-1)

    def store(ref, value, c=None):
        if c is None:
            ref[...] = value().astype(BF16)
        else:
            ref[:, c:c + W_A] = value().astype(BF16)

    def kv_b():
        kv = proj(OFF_KB, 2 * W_KV_B)
        kb_ref[...] = spread_kv(head_norm(kv[:, :W_KV_B], gains_ref[1:2, :W_KV_B]))
        vbt_ref[...] = kv[:, W_KV_B:].T.astype(BF16)

    early = [
        lambda: store(ka_ref, lambda: head_norm_t(proj(OFF_KA, W_A).T).T * gains_ref[0:1, :]),
        lambda: store(vat_ref, lambda: proj_t(WT_VA, W_A)),
        kv_b,
    ]
    late = [
        lambda: store(qat_ref, lambda: head_norm_t(proj_t(WT_QA, W_A))),
        lambda: store(qbt_ref, lambda: head_norm_t(proj_t(WT_QB, W_B))),
        lambda: store(za_ref, lambda: silu(proj(OFF_ZA, W_A))),
        lambda: store(zb_ref, lambda: silu(proj(OFF_ZB, W_B))),
    ]
    for c in range(0, D_MODEL, W_A):
        late.append(lambda c=c: store(ga_ref, lambda: sigmoid(proj(OFF_GA + c, W_A)), c))
        late.append(lambda c=c: store(gb_ref, lambda: sigmoid(proj(OFF_GB + c, W_A)), c))
    return early, late


def _reduce_keys(x, combine, reduce):
    while x.shape[0] % (2 * SUBLANES) == 0:
        half = x.shape[0] // 2
        x = combine(x[:half], x[half:])
    return reduce(x, axis=0, keepdims=True)


def _attend_t(s_t, v_t, sink=None):
    m = _reduce_keys(s_t, jnp.maximum, jnp.max)
    if sink is not None:
        m = jnp.maximum(m, sink)
    p = jnp.exp2((s_t - m).astype(BF16))
    v_ones = jnp.concatenate([v_t, jnp.ones((ONES_ROWS, v_t.shape[1]), BF16)], axis=0)
    o = jnp.dot(v_ones, p, preferred_element_type=F32)
    l = o[HEAD_DIM:HEAD_DIM + 1]
    if sink is not None:
        l = l + jnp.exp2(sink - m)
    return o[:HEAD_DIM] * (1.0 / l)


def _head_rows(qt_ref, g, j, queries):
    lo = g * LANE_GROUP + j * HEAD_DIM
    head = qt_ref[lo:lo + HEAD_DIM, queries]
    above = jnp.zeros((j * HEAD_DIM, head.shape[1]), BF16)
    below = jnp.zeros(((HEADS_PER_GROUP - 1 - j) * HEAD_DIM, head.shape[1]), BF16)
    return jnp.concatenate([part for part in (above, head, below) if part.shape[0]], axis=0)


def _qk_t(keys, qt_masked, chunk):
    parts = [jnp.dot(keys[r:r + chunk], qt_masked, preferred_element_type=F32)
             for r in range(0, keys.shape[0], chunk)]
    return jnp.concatenate(parts, axis=0)


def _na_blocks(qt_ref, k_refs, vt_refs, bias_ref, case):
    blocks = []
    for g in range(W_A // LANE_GROUP):
        cols = slice(g * LANE_GROUP, (g + 1) * LANE_GROUP)
        for j in range(HEADS_PER_GROUP):
            h = g * HEADS_PER_GROUP + j

            def logits(g=g, cols=cols, j=j):
                k4 = jnp.concatenate([r[:, cols] for r in k_refs], axis=0)
                return _qk_t(k4, _head_rows(qt_ref, g, j, slice(None)), NA_QK_ROWS)

            def finish(s_t, h=h):
                feat = slice(h * HEAD_DIM, (h + 1) * HEAD_DIM)
                v_t = jnp.concatenate([r[feat, :] for r in vt_refs], axis=1)
                return _attend_t(s_t + bias_ref[case, h].astype(F32), v_t)

            blocks.append((logits, finish))
    return blocks


def _win_blocks(qt_ref, k_refs, vt_refs, bias_ref, sink_ref, t, nt):
    n_sub = TQ // BLOCK
    blocks = []
    for g in range(KVH_B):
        cols = slice(g * LANE_GROUP, (g + 1) * LANE_GROUP)
        for sub in range(n_sub):
            rows = slice(sub * BLOCK, (sub + 1) * BLOCK)
            key_lo = TQ + (sub - 1) * BLOCK
            if sub == 0:
                case = jnp.where(t == 0, 0, 1)
            elif sub == n_sub - 1:
                case = jnp.where(t == nt - 1, 2, 1)
            else:
                case = 1
            for pair in range(G_B // WIN_STACK):

                def logits(g=g, cols=cols, rows=rows, key_lo=key_lo, pair=pair):
                    kwin = jnp.concatenate([r[:, cols] for r in k_refs], axis=0)
                    heads = range(pair * WIN_STACK, (pair + 1) * WIN_STACK)
                    qt_pair = jnp.concatenate([_head_rows(qt_ref, g, j, rows) for j in heads], axis=1)
                    return _qk_t(kwin[key_lo:key_lo + 3 * BLOCK], qt_pair, WIN_QK_ROWS)

                def finish(s_t, g=g, key_lo=key_lo, pair=pair, case=case):
                    vwin_t = jnp.concatenate([r[g * HEAD_DIM:(g + 1) * HEAD_DIM, :] for r in vt_refs], axis=1)
                    v3_t = vwin_t[:, key_lo:key_lo + 3 * BLOCK]
                    return _attend_t(s_t + bias_ref[case, g, pair].astype(F32), v3_t,
                                     sink_ref[g * (G_B // WIN_STACK) + pair])

                blocks.append((logits, finish))
    return blocks


def _run_interleaved(blocks, fillers):
    outs, pending = [], []
    fillers = list(fillers)
    every = max(1, len(blocks) // max(1, len(fillers)))
    for i, (logits, finish) in enumerate(blocks):
        pending.append((logits(), finish))
        if len(pending) > LOOKAHEAD:
            s_t, fin = pending.pop(0)
            outs.append(fin(s_t))
        if fillers and i % every == every - 1:
            fillers.pop(0)()
    outs += [fin(s_t) for s_t, fin in pending]
    for filler in fillers:
        filler()
    return outs


def _assemble_win(outs):
    n_sub, n_pair = TQ // BLOCK, G_B // WIN_STACK
    feats = []
    for g in range(KVH_B):
        per_sub = []
        for sub in range(n_sub):
            pairs = outs[(g * n_sub + sub) * n_pair:(g * n_sub + sub + 1) * n_pair]
            heads = [o[:, i * BLOCK:(i + 1) * BLOCK] for o in pairs for i in range(WIN_STACK)]
            per_sub.append(jnp.concatenate(heads, axis=0))
        feats.append(jnp.concatenate(per_sub, axis=1))
    return jnp.concatenate(feats, axis=0)


def _layer_kernel(xp_ref, xr_ref, w_ref, wt_ref, bd_ref, gains_ref, bias_a_ref, bias_b_ref, sink_ref,
                  woa_ref, wob_ref, wout_ref, y_ref,
                  ka_ring, vat_ring, kb_ring, vbt_ring, qat_buf, za_buf, qbt_buf, zb_buf, ga_buf, gb_buf, *, nt):
    s = pl.program_id(0)
    kv_rings = (ka_ring, vat_ring, kb_ring, vbt_ring)
    q_bufs = (qat_buf, za_buf, qbt_buf, zb_buf, ga_buf, gb_buf)

    @pl.when(s == 0)
    def _():
        for buf in kv_rings + q_bufs:
            buf[...] = jnp.zeros(buf.shape, buf.dtype)

    kv_slot = lambda tile: lax.rem(tile + 2 * KV_SLOTS, KV_SLOTS)
    q_slot = lambda tile: lax.rem(tile + 2 * Q_SLOTS, Q_SLOTS)

    fillers = []
    for u in range(STEP_TILES):
        tile = s * STEP_TILES + u
        early, late = _project_tile(
            xp_ref.at[u * TQ:(u + 1) * TQ, :], w_ref, wt_ref, bd_ref, gains_ref,
            *[ring.at[kv_slot(tile)] for ring in kv_rings],
            qat_buf.at[q_slot(tile)], za_buf.at[q_slot(tile)], qbt_buf.at[q_slot(tile)], zb_buf.at[q_slot(tile)],
            ga_buf.at[q_slot(tile)], gb_buf.at[q_slot(tile)])
        if u == 0:
            for chunk in early:
                chunk()
            fillers += late
        else:
            fillers += early + late

    blocks, attended = [], []
    for u in range(STEP_TILES):
        tile = (s - 1) * STEP_TILES + u
        t = lax.rem(jnp.maximum(tile, 0), nt)
        case = jnp.where(t == 0, 0, jnp.where(t == nt - 1, 2, 1))
        window = [kv_slot(tile + d) for d in (-1, 0, 1)]
        na = _na_blocks(qat_buf.at[q_slot(tile)], [ka_ring.at[i] for i in window],
                        [vat_ring.at[i] for i in window], bias_a_ref, case)
        win = _win_blocks(qbt_buf.at[q_slot(tile)], [kb_ring.at[i] for i in window],
                          [vbt_ring.at[i] for i in window], bias_b_ref, sink_ref, t, nt)
        attended.append((q_slot(tile), len(blocks), len(na), len(win)))
        blocks += na + win
    outs = _run_interleaved(blocks, fillers)

    oa, ob, ga, gb = [], [], [], []
    for slot, lo, n_na, n_win in attended:
        oa.append((jnp.concatenate(outs[lo:lo + n_na], axis=0).T * za_buf[slot].astype(F32)).astype(BF16))
        ob.append((_assemble_win(outs[lo + n_na:lo + n_na + n_win]).T * zb_buf[slot].astype(F32)).astype(BF16))
        ga.append(ga_buf[slot].astype(F32))
        gb.append(gb_buf[slot].astype(F32))
    rows = lambda parts: jnp.concatenate(parts, axis=0)
    a = jnp.dot(rows(oa), woa_ref[...], preferred_element_type=F32)
    b = jnp.dot(rows(ob), wob_ref[...], preferred_element_type=F32)
    merged = rows(ga) * a + rows(gb) * b
    y_ref[...] = xr_ref[...] + jnp.dot(merged.astype(BF16), wout_ref[...], preferred_element_type=F32)


def _layer_call(x2d, w_bf, w_t, bd, gains, bias_a, bias_b, sink, woa, wob, wout, seq):
    n_steps = x2d.shape[0] // (TQ * STEP_TILES)
    nt = seq // TQ
    assert seq % TQ == 0 and nt >= 3 and x2d.shape[0] % (TQ * STEP_TILES) == 0
    proj_step = lambda s: (jnp.minimum(s, n_steps - 1), 0)
    attn_step = lambda s: (jnp.maximum(s - 1, 0), 0)
    consts = (w_bf, w_t, bd, gains, bias_a, bias_b, sink, woa, wob, wout)
    ring = lambda slots, rows, cols: pltpu.VMEM((slots, rows, cols), BF16)
    block = (TQ * STEP_TILES, D_MODEL)
    return pl.pallas_call(
        functools.partial(_layer_kernel, nt=nt),
        grid=(n_steps + 1,),
        in_specs=[pl.BlockSpec(block, proj_step), pl.BlockSpec(block, attn_step)]
                 + [_const_spec(c.shape) for c in consts],
        out_specs=pl.BlockSpec(block, attn_step),
        out_shape=jax.ShapeDtypeStruct(x2d.shape, F32),
        scratch_shapes=[ring(KV_SLOTS, TQ, W_A), ring(KV_SLOTS, W_A, TQ), ring(KV_SLOTS, TQ, W_B),
                        ring(KV_SLOTS, W_KV_B, TQ), ring(Q_SLOTS, W_A, TQ), ring(Q_SLOTS, TQ, W_A),
                        ring(Q_SLOTS, W_B, TQ), ring(Q_SLOTS, TQ, W_B), ring(Q_SLOTS, TQ, D_MODEL),
                        ring(Q_SLOTS, TQ, D_MODEL)],
        compiler_params=pltpu.CompilerParams(dimension_semantics=("arbitrary",),
                                             vmem_limit_bytes=VMEM_LIMIT),
        name="encoder_layer",
    )(x2d, x2d, *consts)


def _toeplitz(v, m, n):
    w = v.shape[-1]
    assert m <= w and n <= w
    doubled = jnp.concatenate([v, v], axis=-1)
    return jnp.stack([doubled[..., w - q:w - q + n] for q in range(m)], axis=-2)


def _na_bias(rpb, rows):
    kr = min(NA_ROWS, rows)
    assert kr == NA_ROWS and rows >= 3 * ROWS_PER_TILE
    n_key_rows = KEY_TILES * ROWS_PER_TILE
    rpb = rpb.astype(F32) * LOG2E
    ring_pad = jnp.zeros(rpb.shape[:-1] + (2 * GRID_W - (2 * NA_COLS - 1),), F32)
    ring = jnp.concatenate([rpb[..., NA_COLS - 1:], ring_pad, rpb[..., :NA_COLS - 1]], axis=-1)
    col = _toeplitz(ring, GRID_W, GRID_W).astype(BIAS_DTYPE)
    first = NA_ROWS - 1 - ROWS_PER_TILE
    per_row = [col[:, first - i:first - i + n_key_rows] for i in range(ROWS_PER_TILE)]
    vals_t = jnp.transpose(jnp.stack(per_row, axis=1), (0, 2, 4, 1, 3)).reshape(H_A, KEY_TILES * TQ, TQ)
    i = np.arange(ROWS_PER_TILE)[:, None, None, None]
    qc = np.arange(GRID_W)[None, :, None, None]
    j = np.arange(n_key_rows)[None, None, :, None]
    kc = np.arange(GRID_W)[None, None, None, :]
    cs = np.clip(qc - NA_COLS // 2, 0, GRID_W - NA_COLS)
    col_ok = (kc >= cs) & (kc < cs + NA_COLS)
    shape = (ROWS_PER_TILE, GRID_W, n_key_rows, GRID_W)
    row_ok = [
        (j >= ROWS_PER_TILE) & (j < ROWS_PER_TILE + kr),
        (j - i >= 0) & (j - i < kr),
        (j >= 0) & (j < kr),
    ]
    out = []
    for ok in row_ok:
        valid_t = np.broadcast_to(ok & col_ok, shape).transpose(2, 3, 0, 1).reshape(KEY_TILES * TQ, TQ)
        out.append(jnp.where(valid_t[None], vals_t, jnp.asarray(NEG, BIAS_DTYPE)))
    return jnp.stack(out)


def _t5_bucket(rel):
    half = T5_BUCKETS // 2
    max_exact = half // 2
    n = jnp.abs(rel)
    large = max_exact + (jnp.log(jnp.maximum(n, 1).astype(jnp.float32) / max_exact)
                         / math.log(T5_MAX_DIST / max_exact) * (half - max_exact)).astype(jnp.int32)
    large = jnp.minimum(large, half - 1)
    return jnp.where(rel > 0, half, 0) + jnp.where(n < max_exact, n, large)


def _win_bias(t5_table):
    ring = 4 * BLOCK
    u = jnp.arange(ring)
    rel = jnp.where(u < ring // 2, u, u - ring)
    onehot = (_t5_bucket(rel)[:, None] == jnp.arange(T5_BUCKETS)[None, :]).astype(F32)
    by_rel = jnp.dot(onehot, t5_table.astype(F32) * LOG2E, precision=lax.Precision.HIGHEST)
    by_rel = jnp.where((jnp.abs(rel) <= WINDOW)[:, None], by_rel, NEG).T
    bias = _toeplitz(jnp.roll(by_rel, BLOCK, axis=-1), BLOCK, 3 * BLOCK)
    blk = np.arange(3 * BLOCK) // BLOCK
    cases = [jnp.where((blk != 0)[None, None], bias, NEG), bias, jnp.where((blk != 2)[None, None], bias, NEG)]
    shape = (KVH_B, G_B // WIN_STACK, WIN_STACK * BLOCK, 3 * BLOCK)
    return jnp.stack([jnp.swapaxes(c.reshape(shape), -1, -2) for c in cases]).astype(BIAS_DTYPE)


def _layer_constants(qn_a, kn_a, qn_b, kn_b):
    q_scale = LOG2E / math.sqrt(HEAD_DIM)
    tile = lambda gvec: jnp.tile(gvec.astype(F32), W_A // HEAD_DIM)
    gains = jnp.stack([tile(qn_a) * tile(kn_a) * q_scale, tile(qn_b) * tile(kn_b) * q_scale])
    lane = np.arange(LANE_GROUP)
    bd = jnp.asarray(lane[:, None] // HEAD_DIM == lane[None, :] // HEAD_DIM, BF16)
    return gains, bd


def _encoder_layer(x, norm_g, w_in, qn_a, kn_a, rpb_a, qn_b, kn_b, sink_b, w_o_a, w_o_b, w_out, t5_table):
    batch, seq, d = x.shape
    assert d == D_MODEL and seq % GRID_W == 0
    x2d = x.reshape(batch * seq, d)
    gains, bd = _layer_constants(qn_a, kn_a, qn_b, kn_b)
    w_bf = (norm_g.astype(F32)[:, None] * w_in).astype(BF16)
    sink = jnp.repeat(sink_b.astype(F32) * LOG2E, BLOCK).reshape(H_B // WIN_STACK, 1, WIN_STACK * BLOCK)
    w_t = jnp.concatenate([w_bf[:, OFF_QA:OFF_QA + W_A], w_bf[:, OFF_VA:OFF_VA + W_A],
                           w_bf[:, OFF_QB:OFF_QB + W_B]], axis=1).T
    y = _layer_call(x2d, w_bf, w_t, bd, gains, _na_bias(rpb_a, seq // GRID_W), _win_bias(t5_table), sink,
                    w_o_a.astype(BF16), w_o_b.astype(BF16), w_out.astype(BF16), seq)
    return y.reshape(batch, seq, d)


def kernel(x_prompt, x_sample, norm_g, w_in, qn_a, kn_a, rpb_a, qn_b, kn_b, sink_b, w_o_a, w_o_b, w_out, t5_table):
    y_prompt, y_sample = x_prompt, x_sample
    for l in range(norm_g.shape[0]):
        params = (norm_g[l], w_in[l], qn_a[l], kn_a[l], rpb_a[l], qn_b[l], kn_b[l], sink_b[l],
                  w_o_a[l], w_o_b[l], w_out[l], t5_table)
        y_prompt = _encoder_layer(y_prompt, *params)
        y_sample = _encoder_layer(y_sample, *params)
    return (y_prompt, y_sample)
```
